```python
import math, functools
import jax, jax.numpy as jnp
from jax import lax
import numpy as np

D_MODEL = 2048
BATCH = 4
SEQ = 8192
DEPTH = 1
DEC_BATCH = 8
DEC_SEQ = 16
PAST_LEN = 1024

CHUNK = 64
Q_BLOCK = 128
HEAD_DIM = 128
V_DIM = 2 * HEAD_DIM
N_HEADS = D_MODEL // (2 * HEAD_DIM)
QK_WIDTH = N_HEADS * 2 * HEAD_DIM
ATTN_WIDTH = N_HEADS * V_DIM
GM_CHUNK = 128
GM_GROUPS = 8
GM_WIDTH = D_MODEL
GM_GROUP_DIM = GM_WIDTH // GM_GROUPS
D_FF = -(-8 * D_MODEL // (3 * 256)) * 256
IN_WIDTH = 2 * QK_WIDTH + ATTN_WIDTH + 2 * GM_WIDTH + 2 * D_MODEL
NORM_EPS = 1e-6
SUBLN_EPS = 1e-5

kernel_name = 'streaming_diffattn_gmlp_hybrid'


def _rmsnorm(x, w, eps=NORM_EPS):
    xf = x.astype(jnp.float32)
    y = xf * lax.rsqrt(jnp.mean(xf * xf, axis=-1, keepdims=True) + eps)
    return (y * w.astype(jnp.float32)).astype(x.dtype)


def _layernorm(x, w, b, eps=NORM_EPS):
    xf = x.astype(jnp.float32)
    xc = xf - jnp.mean(xf, axis=-1, keepdims=True)
    var = jnp.mean(xc * xc, axis=-1, keepdims=True)
    y = xc * lax.rsqrt(var + eps) * w.astype(jnp.float32) + b.astype(jnp.float32)
    return y.astype(x.dtype)


def _alibi_slopes():
    return 2.0 ** (-8.0 * jnp.arange(1, N_HEADS + 1, dtype=jnp.float32) / N_HEADS)


def _diff_attend(q, k, v, t_pos, s_pos, lam):
    s = jnp.einsum('bqhmd,bkhmd->bhmqk', q.astype(jnp.float32), k.astype(jnp.float32))
    dist = jnp.abs(t_pos[:, None] - s_pos[None, :]).astype(jnp.float32)
    bias = -_alibi_slopes()[:, None, None] * dist
    allowed = (s_pos[None, :] // CHUNK) <= (t_pos[:, None] // CHUNK)
    s = jnp.where(allowed, s + bias[None, :, None], -jnp.inf)
    p = jax.nn.softmax(s, axis=-1)
    a = p[:, :, 0] - lam * p[:, :, 1]
    return jnp.einsum('bhqk,bkhe->bqhe', a, v.astype(jnp.float32))


def _attend_prompt(q, k, v, lam):
    b, s = q.shape[0], q.shape[1]
    nb = s // Q_BLOCK
    q_blocks = q.reshape(b, nb, Q_BLOCK, N_HEADS, 2, HEAD_DIM).transpose(1, 0, 2, 3, 4, 5)
    s_pos = jnp.arange(s)

    def one_block(args):
        q_blk, i = args
        t_pos = i * Q_BLOCK + jnp.arange(Q_BLOCK)
        return _diff_attend(q_blk, k, v, t_pos, s_pos, lam)

    o = lax.map(one_block, (q_blocks, jnp.arange(nb)))
    return o.transpose(1, 0, 2, 3, 4).reshape(b, s, N_HEADS, V_DIM)


def _attend_sample(q, k, v, lam, cache_k, cache_v):
    past, n = cache_k.shape[1], q.shape[1]
    k_all = jnp.concatenate([cache_k.astype(k.dtype), k], axis=1)
    v_all = jnp.concatenate([cache_v.astype(v.dtype), v], axis=1)
    t_pos = past + jnp.arange(n)
    s_pos = jnp.arange(past + n)
    return _diff_attend(q, k_all, v_all, t_pos, s_pos, lam)


def _spatial_gate(vn, ws, bs):
    n = vn.shape[-3]
    ws_m = jnp.tril(ws[:, :n, :n])
    mixed = jnp.einsum('gts,...sgd->...tgd', ws_m, vn)
    return mixed + bs[:, :n].T[:, :, None]


def _layer(x, lp, lambda_init, attend, chunked):
    b, s, _ = x.shape
    h = _rmsnorm(x, lp['norm_mix_pre'])
    z = h @ lp['w_in']
    splits = np.cumsum([QK_WIDTH, QK_WIDTH, ATTN_WIDTH, GM_WIDTH, GM_WIDTH, D_MODEL]).tolist()
    zq, zk, zv, zu, zg, za, zb = jnp.split(z, splits, axis=-1)

    q = zq.reshape(b, s, N_HEADS, 2, HEAD_DIM) * (HEAD_DIM ** -0.5)
    k = zk.reshape(b, s, N_HEADS, 2, HEAD_DIM)
    v = zv.reshape(b, s, N_HEADS, V_DIM)
    lam = (jnp.exp(jnp.sum(lp['lambda_q1'].astype(jnp.float32) * lp['lambda_k1'].astype(jnp.float32)))
           - jnp.exp(jnp.sum(lp['lambda_q2'].astype(jnp.float32) * lp['lambda_k2'].astype(jnp.float32)))
           + lambda_init)
    o = attend(q, k, v, lam)
    o = _rmsnorm(o, lp['subln_w'], SUBLN_EPS) * (1.0 - lambda_init)
    attn_out = o.reshape(b, s, ATTN_WIDTH).astype(x.dtype)

    u = jax.nn.gelu(zu, approximate=False)
    vn = _layernorm(jax.nn.gelu(zg, approximate=False), lp['gm_ln_w'], lp['gm_ln_b'])
    vn = vn.reshape(b, s, GM_GROUPS, GM_GROUP_DIM)
    if chunked:
        vc = vn.reshape(b, s // GM_CHUNK, GM_CHUNK, GM_GROUPS, GM_GROUP_DIM)
        mixed = _spatial_gate(vc, lp['gm_ws'], lp['gm_bs']).reshape(b, s, GM_WIDTH)
    else:
        mixed = _spatial_gate(vn, lp['gm_ws'], lp['gm_bs']).reshape(b, s, GM_WIDTH)
    gm_out = u * mixed

    merged = (jax.nn.sigmoid(za) * (attn_out @ lp['w_branch_attn'])
              + jax.nn.sigmoid(zb) * (gm_out @ lp['w_branch_gmlp']))
    x = x + _rmsnorm(merged @ lp['w_out'], lp['norm_mix_post'])

    h2 = _rmsnorm(x, lp['norm_ffn_pre'])
    f = (jax.nn.silu(h2 @ lp['w_ffn_gate']) * (h2 @ lp['w_ffn_up'])) @ lp['w_ffn_down']
    x = x + _rmsnorm(f, lp['norm_ffn_post'])
    return x, k, v, vn


def setup_inputs(seed: int = 0) -> dict:
    key = jax.random.key(seed)
    ks = jax.random.split(key, 24)

    def nrm(k, shape, scale=1.0):
        return scale * jax.random.normal(k, shape, jnp.float32)

    return {
        'x_prompt': nrm(ks[0], (BATCH, SEQ, D_MODEL)),
        'x_sample': nrm(ks[1], (DEC_BATCH, DEC_SEQ, D_MODEL)),
        'cache_k': nrm(ks[2], (DEPTH, DEC_BATCH, PAST_LEN, N_HEADS, 2, HEAD_DIM)),
        'cache_v': nrm(ks[3], (DEPTH, DEC_BATCH, PAST_LEN, N_HEADS, V_DIM)),
        'norm_mix_pre': 1.0 + nrm(ks[4], (DEPTH, D_MODEL), 0.05),
        'norm_mix_post': 1.0 + nrm(ks[5], (DEPTH, D_MODEL), 0.05),
        'w_in': nrm(ks[6], (DEPTH, D_MODEL, IN_WIDTH), D_MODEL ** -0.5),
        'lambda_q1': nrm(ks[7], (DEPTH, HEAD_DIM), 0.1),
        'lambda_k1': nrm(ks[8], (DEPTH, HEAD_DIM), 0.1),
        'lambda_q2': nrm(ks[9], (DEPTH, HEAD_DIM), 0.1),
        'lambda_k2': nrm(ks[10], (DEPTH, HEAD_DIM), 0.1),
        'subln_w': 1.0 + nrm(ks[11], (DEPTH, V_DIM), 0.05),
        'gm_ln_w': 1.0 + nrm(ks[12], (DEPTH, GM_WIDTH), 0.05),
        'gm_ln_b': nrm(ks[13], (DEPTH, GM_WIDTH), 0.02),
        'gm_ws': nrm(ks[14], (DEPTH, GM_GROUPS, GM_CHUNK, GM_CHUNK), GM_CHUNK ** -0.5),
        'gm_bs': 1.0 + nrm(ks[15], (DEPTH, GM_GROUPS, GM_CHUNK), 0.05),
        'w_branch_attn': nrm(ks[16], (DEPTH, ATTN_WIDTH, D_MODEL), ATTN_WIDTH ** -0.5),
        'w_branch_gmlp': nrm(ks[17], (DEPTH, GM_WIDTH, D_MODEL), GM_WIDTH ** -0.5),
        'w_out': nrm(ks[18], (DEPTH, D_MODEL, D_MODEL), D_MODEL ** -0.5),
        'norm_ffn_pre': 1.0 + nrm(ks[19], (DEPTH, D_MODEL), 0.05),
        'norm_ffn_post': 1.0 + nrm(ks[20], (DEPTH, D_MODEL), 0.05),
        'w_ffn_gate': nrm(ks[21], (DEPTH, D_MODEL, D_FF), D_MODEL ** -0.5),
        'w_ffn_up': nrm(ks[22], (DEPTH, D_MODEL, D_FF), D_MODEL ** -0.5),
        'w_ffn_down': nrm(ks[23], (DEPTH, D_FF, D_MODEL), D_FF ** -0.5),
    }


def reference(x_prompt, x_sample, cache_k, cache_v, norm_mix_pre, norm_mix_post, w_in,
              lambda_q1, lambda_k1, lambda_q2, lambda_k2, subln_w, gm_ln_w, gm_ln_b, gm_ws, gm_bs,
              w_branch_attn, w_branch_gmlp, w_out, norm_ffn_pre, norm_ffn_post,
              w_ffn_gate, w_ffn_up, w_ffn_down):
    y_prompt, y_sample = x_prompt, x_sample
    k_prompt_rows, v_prompt_rows, k_sample_rows, v_sample_rows, gm_sample_rows = [], [], [], [], []
    for l in range(DEPTH):
        lp = dict(norm_mix_pre=norm_mix_pre[l], norm_mix_post=norm_mix_post[l], w_in=w_in[l],
                  lambda_q1=lambda_q1[l], lambda_k1=lambda_k1[l], lambda_q2=lambda_q2[l],
                  lambda_k2=lambda_k2[l], subln_w=subln_w[l], gm_ln_w=gm_ln_w[l], gm_ln_b=gm_ln_b[l],
                  gm_ws=gm_ws[l], gm_bs=gm_bs[l], w_branch_attn=w_branch_attn[l],
                  w_branch_gmlp=w_branch_gmlp[l], w_out=w_out[l], norm_ffn_pre=norm_ffn_pre[l],
                  norm_ffn_post=norm_ffn_post[l], w_ffn_gate=w_ffn_gate[l], w_ffn_up=w_ffn_up[l],
                  w_ffn_down=w_ffn_down[l])
        lambda_init = 0.8 - 0.6 * math.exp(-0.3 * l)
        y_prompt, kp, vp, _ = _layer(y_prompt, lp, lambda_init, _attend_prompt, True)
        attend_s = functools.partial(_attend_sample, cache_k=cache_k[l], cache_v=cache_v[l])
        y_sample, ks_, vs_, gs_ = _layer(y_sample, lp, lambda_init, attend_s, False)
        k_prompt_rows.append(kp)
        v_prompt_rows.append(vp)
        k_sample_rows.append(ks_)
        v_sample_rows.append(vs_)
        gm_sample_rows.append(gs_)
    return (y_prompt, y_sample, jnp.stack(k_prompt_rows), jnp.stack(v_prompt_rows),
            jnp.stack(k_sample_rows), jnp.stack(v_sample_rows), jnp.stack(gm_sample_rows))
```

```python
import functools
import math

import jax
import jax.numpy as jnp
import numpy as np
from jax import lax
from jax.experimental import pallas as pl
from jax.experimental.pallas import tpu as pltpu

CHUNK = 64
HEAD_DIM = 128
V_DIM = 2 * HEAD_DIM
GM_CHUNK = 128
GM_GROUPS = 8
NORM_EPS = 1e-6
SUBLN_EPS = 1e-5
LOG2E = math.log2(math.e)
NEG_BIG = -1e30

V7X_VMEM_BYTES = 64 * 1024 * 1024
VMEM_LIMIT_BYTES = 56 * 1024 * 1024

BF16 = jnp.bfloat16
F32 = jnp.float32


def _cparams(*sem):
    return pltpu.CompilerParams(dimension_semantics=sem, vmem_limit_bytes=VMEM_LIMIT_BYTES)


def _rms(xf, w, eps):
    return xf * lax.rsqrt(jnp.mean(xf * xf, axis=-1, keepdims=True) + eps) * w


def _gelu(x):
    return 0.5 * x * (1.0 + lax.erf(x * np.float32(np.sqrt(0.5))))


def _dot(a, b):
    return jnp.dot(a, b, preferred_element_type=F32)


def _dot_nt(a, b):
    return lax.dot_general(a, b, (((1,), (1,)), ((), ())), preferred_element_type=F32)


def _norm_cast_kernel(x_ref, w_ref, o_ref):
    o_ref[...] = _rms(x_ref[...], w_ref[...], NORM_EPS).astype(BF16)


def _norm_cast(x, w, tm):
    t, d = x.shape
    return pl.pallas_call(
        _norm_cast_kernel,
        grid=(t // tm,),
        in_specs=[pl.BlockSpec((tm, d), lambda i: (i, 0)),
                  pl.BlockSpec((1, d), lambda i: (0, 0))],
        out_specs=pl.BlockSpec((tm, d), lambda i: (i, 0)),
        out_shape=jax.ShapeDtypeStruct((t, d), BF16),
        compiler_params=_cparams("arbitrary"),
        name="norm_cast",
    )(x, w.reshape(1, d))


def _proj_kernel(h_ref, w_ref, *o_refs, scale, want_f32, want_bf16):
    z = _dot(h_ref[...], w_ref[...])
    if scale is not None:
        z = z * scale
    i = 0
    if want_f32:
        o_refs[i][...] = z
        i += 1
    if want_bf16:
        o_refs[i][...] = z.astype(BF16)


def _proj(h, w, col0, ncols, tm, tn, *, scale=None, want_f32=False, want_bf16=True):
    t, k = h.shape
    off = col0 // tn
    out_shape, out_specs = [], []
    if want_f32:
        out_shape.append(jax.ShapeDtypeStruct((t, ncols), F32))
        out_specs.append(pl.BlockSpec((tm, tn), lambda n, m: (m, n)))
    if want_bf16:
        out_shape.append(jax.ShapeDtypeStruct((t, ncols), BF16))
        out_specs.append(pl.BlockSpec((tm, tn), lambda n, m: (m, n)))
    return pl.pallas_call(
        functools.partial(_proj_kernel, scale=scale, want_f32=want_f32, want_bf16=want_bf16),
        grid=(ncols // tn, t // tm),
        in_specs=[pl.BlockSpec((tm, k), lambda n, m: (m, 0)),
                  pl.BlockSpec((k, tn), lambda n, m: (0, n + off))],
        out_specs=out_specs,
        out_shape=out_shape,
        compiler_params=_cparams("arbitrary", "arbitrary"),
        name="in_proj",
    )(h, w)


def _lambda_full(lam_ref, lambda_init):
    lv = lam_ref[...]
    s1 = jnp.sum(lv[0:1] * lv[1:2], axis=-1, keepdims=True)
    s2 = jnp.sum(lv[2:3] * lv[3:4], axis=-1, keepdims=True)
    return jnp.exp(s1) - jnp.exp(s2) + lambda_init


def _attn_prompt_kernel(slopes_ref, q_ref, k_ref, v_ref, lam_ref, sw_ref, o_ref,
                        bias_ref, m_ref, l_ref, acc_ref, *, tq, lambda_init):
    h = pl.program_id(1)
    qi = pl.program_id(2)
    slope2 = slopes_ref[h] * LOG2E

    @pl.when(qi == 0)
    def _():
        r = lax.broadcasted_iota(jnp.int32, (tq, tq), 0)
        c = lax.broadcasted_iota(jnp.int32, (tq, tq), 1)
        bias_ref[...] = (c - r).astype(F32) * slope2

    def scores(mp, j):
        qm = q_ref[0, :, mp * HEAD_DIM:(mp + 1) * HEAD_DIM]
        km = k_ref[0, pl.ds(pl.multiple_of(j * tq, tq), tq), mp * HEAD_DIM:(mp + 1) * HEAD_DIM]
        return _dot_nt(qm, km)

    def vblk(j):
        return v_ref[0, pl.ds(pl.multiple_of(j * tq, tq), tq), :]

    r = lax.broadcasted_iota(jnp.int32, (tq, tq), 0)
    c = lax.broadcasted_iota(jnp.int32, (tq, tq), 1)
    allowed = (c // CHUNK) <= (r // CHUNK)
    dbias = jnp.abs(r - c).astype(F32) * (-slope2)
    vd = vblk(qi)
    for mp in range(2):
        s = jnp.where(allowed, scores(mp, qi) + dbias, NEG_BIG)
        m = jnp.max(s, axis=-1, keepdims=True)
        p = jnp.exp2(s - m)
        m_ref[mp] = m
        l_ref[mp] = jnp.sum(p, axis=-1, keepdims=True)
        acc_ref[mp] = _dot(p.astype(BF16), vd)

    def body(j, carry):
        cj = slope2 * ((qi - j) * tq).astype(F32)
        vj = vblk(j)
        for mp in range(2):
            s = scores(mp, j) + bias_ref[...]
            m_old = m_ref[mp]
            m_new = jnp.maximum(m_old, jnp.max(s, axis=-1, keepdims=True) - cj)
            p = jnp.exp2(s - (m_new + cj))
            alpha = jnp.exp2(m_old - m_new)
            m_ref[mp] = m_new
            l_ref[mp] = alpha * l_ref[mp] + jnp.sum(p, axis=-1, keepdims=True)
            acc_ref[mp] = alpha * acc_ref[mp] + _dot(p.astype(BF16), vj)
        return carry

    lax.fori_loop(0, qi, body, 0)

    lam = _lambda_full(lam_ref, lambda_init)
    o = acc_ref[0] / l_ref[0] - lam * (acc_ref[1] / l_ref[1])
    o_ref[0] = (_rms(o, sw_ref[...], SUBLN_EPS) * (1.0 - lambda_init)).astype(BF16)


def _attn_prompt(q, k, v, slopes, lam_vecs, subln_w, n_heads, lambda_init, tq):
    b, s, _ = q.shape
    grid_spec = pltpu.PrefetchScalarGridSpec(
        num_scalar_prefetch=1,
        grid=(b, n_heads, s // tq),
        in_specs=[pl.BlockSpec((1, tq, V_DIM), lambda bi, hi, qi, sl: (bi, qi, hi)),
                  pl.BlockSpec((1, s, V_DIM), lambda bi, hi, qi, sl: (bi, 0, hi)),
                  pl.BlockSpec((1, s, V_DIM), lambda bi, hi, qi, sl: (bi, 0, hi)),
                  pl.BlockSpec((4, HEAD_DIM), lambda bi, hi, qi, sl: (0, 0)),
                  pl.BlockSpec((1, V_DIM), lambda bi, hi, qi, sl: (0, 0))],
        out_specs=pl.BlockSpec((1, tq, V_DIM), lambda bi, hi, qi, sl: (bi, qi, hi)),
        scratch_shapes=[pltpu.VMEM((tq, tq), F32),
                        pltpu.VMEM((2, tq, 1), F32),
                        pltpu.VMEM((2, tq, 1), F32),
                        pltpu.VMEM((2, tq, V_DIM), F32)],
    )
    return pl.pallas_call(
        functools.partial(_attn_prompt_kernel, tq=tq, lambda_init=lambda_init),
        grid_spec=grid_spec,
        out_shape=jax.ShapeDtypeStruct(q.shape, BF16),
        compiler_params=_cparams("arbitrary", "arbitrary", "arbitrary"),
        name="attn_prompt",
    )(slopes, q, k, v, lam_vecs, subln_w.reshape(1, V_DIM))


def _attn_sample_kernel(slopes_ref, q_ref, kn_ref, vn_ref, kc_ref, vc_ref, lam_ref, sw_ref, o_ref,
                        *, past, n, lambda_init):
    h = pl.program_id(1)
    slope2 = slopes_ref[h] * LOG2E
    r_c = lax.broadcasted_iota(jnp.int32, (n, past), 0)
    c_c = lax.broadcasted_iota(jnp.int32, (n, past), 1)
    bias_c = (past + r_c - c_c).astype(F32) * (-slope2)
    r_n = lax.broadcasted_iota(jnp.int32, (n, n), 0)
    c_n = lax.broadcasted_iota(jnp.int32, (n, n), 1)
    bias_n = jnp.abs(r_n - c_n).astype(F32) * (-slope2)
    vc = vc_ref[0].astype(BF16)
    vn = vn_ref[0]
    outs = []
    for mp in range(2):
        sl = slice(mp * HEAD_DIM, (mp + 1) * HEAD_DIM)
        qm = q_ref[0, :, sl]
        s_c = _dot_nt(qm, kc_ref[0, :, sl].astype(BF16)) + bias_c
        s_n = _dot_nt(qm, kn_ref[0, :, sl]) + bias_n
        m = jnp.maximum(jnp.max(s_c, axis=-1, keepdims=True), jnp.max(s_n, axis=-1, keepdims=True))
        p_c = jnp.exp2(s_c - m)
        p_n = jnp.exp2(s_n - m)
        l = jnp.sum(p_c, axis=-1, keepdims=True) + jnp.sum(p_n, axis=-1, keepdims=True)
        acc = _dot(p_c.astype(BF16), vc) + _dot(p_n.astype(BF16), vn)
        outs.append(acc / l)
    lam = _lambda_full(lam_ref, lambda_init)
    o = outs[0] - lam * outs[1]
    o_ref[0] = (_rms(o, sw_ref[...], SUBLN_EPS) * (1.0 - lambda_init)).astype(BF16)


def _attn_sample(q, kn, vn, kc, vc, slopes, lam_vecs, subln_w, n_heads, lambda_init):
    b, n, _ = q.shape
    past = kc.shape[1]
    new_spec = pl.BlockSpec((1, n, V_DIM), lambda bi, hi, sl: (bi, 0, hi))
    cache_spec = pl.BlockSpec((1, past, V_DIM), lambda bi, hi, sl: (bi, 0, hi))
    grid_spec = pltpu.PrefetchScalarGridSpec(
        num_scalar_prefetch=1,
        grid=(b, n_heads),
        in_specs=[new_spec, new_spec, new_spec, cache_spec, cache_spec,
                  pl.BlockSpec((4, HEAD_DIM), lambda bi, hi, sl: (0, 0)),
                  pl.BlockSpec((1, V_DIM), lambda bi, hi, sl: (0, 0))],
        out_specs=new_spec,
    )
    return pl.pallas_call(
        functools.partial(_attn_sample_kernel, past=past, n=n, lambda_init=lambda_init),
        grid_spec=grid_spec,
        out_shape=jax.ShapeDtypeStruct(q.shape, BF16),
        compiler_params=_cparams("arbitrary", "arbitrary"),
        name="attn_sample",
    )(slopes, q, kn, vn, kc, vc, lam_vecs, subln_w.reshape(1, V_DIM))


def _gmlp_kernel(h_ref, wu_ref, wg_ref, lnw_ref, lnb_ref, ws_ref, bs_ref, *refs,
                 tm, width, rows_per_stream, want_vn):
    if want_vn:
        o_ref, vn_out_ref, g_ref = refs
    else:
        o_ref, g_ref = refs
    cb = 512
    gd = width // GM_GROUPS
    h = h_ref[...]
    rowsum = jnp.zeros((tm, 1), F32)
    for j in range(width // cb):
        g = _gelu(_dot(h, wg_ref[:, j * cb:(j + 1) * cb]))
        g_ref[:, j * cb:(j + 1) * cb] = g
        rowsum = rowsum + jnp.sum(g, axis=-1, keepdims=True)
    mean = rowsum / width
    sq = jnp.zeros((tm, 1), F32)
    for j in range(width // cb):
        xc = g_ref[:, j * cb:(j + 1) * cb] - mean
        sq = sq + jnp.sum(xc * xc, axis=-1, keepdims=True)
    rstd = lax.rsqrt(sq / width + NORM_EPS)

    r = lax.broadcasted_iota(jnp.int32, (GM_CHUNK, GM_CHUNK), 0)
    c = lax.broadcasted_iota(jnp.int32, (GM_CHUNK, GM_CHUNK), 1)
    causal = (c <= r) & ((c // rows_per_stream) == (r // rows_per_stream))
    for j in range(width // cb):
        cols = slice(j * cb, (j + 1) * cb)
        vn = (g_ref[:, cols] - mean) * rstd * lnw_ref[:, cols] + lnb_ref[:, cols]
        if want_vn:
            vn_out_ref[:, cols] = vn
        vnb = vn.astype(BF16)
        u = _gelu(_dot(h, wu_ref[:, cols]))
        for gg in range(cb // gd):
            grp = j * (cb // gd) + gg
            wsm = jnp.where(causal, ws_ref[grp], 0.0).astype(BF16)
            bias = bs_ref[grp]
            for ch in range(tm // GM_CHUNK):
                rows = slice(ch * GM_CHUNK, (ch + 1) * GM_CHUNK)
                mixed = _dot(wsm, vnb[rows, gg * gd:(gg + 1) * gd]) + bias
                o_ref[rows, j * cb + gg * gd:j * cb + (gg + 1) * gd] = (
                    u[rows, gg * gd:(gg + 1) * gd] * mixed).astype(BF16)


def _gmlp(h, w, col_u, col_g, lnw, lnb, ws_t, bs_t, tm, rows_per_stream, want_vn):
    t, k = h.shape
    width = lnw.shape[-1]
    out_shape = [jax.ShapeDtypeStruct((t, width), BF16)]
    out_specs = [pl.BlockSpec((tm, width), lambda i: (i, 0))]
    if want_vn:
        out_shape.append(jax.ShapeDtypeStruct((t, width), F32))
        out_specs.append(pl.BlockSpec((tm, width), lambda i: (i, 0)))
    one = pl.Buffered(1)
    res = pl.pallas_call(
        functools.partial(_gmlp_kernel, tm=tm, width=width, rows_per_stream=rows_per_stream,
                          want_vn=want_vn),
        grid=(t // tm,),
        in_specs=[pl.BlockSpec((tm, k), lambda i: (i, 0)),
                  pl.BlockSpec((k, width), lambda i: (0, col_u // width), pipeline_mode=one),
                  pl.BlockSpec((k, width), lambda i: (0, col_g // width), pipeline_mode=one),
                  pl.BlockSpec((1, width), lambda i: (0, 0)),
                  pl.BlockSpec((1, width), lambda i: (0, 0)),
                  pl.BlockSpec((GM_GROUPS, GM_CHUNK, GM_CHUNK), lambda i: (0, 0, 0)),
                  pl.BlockSpec((GM_GROUPS, GM_CHUNK, 1), lambda i: (0, 0, 0))],
        out_specs=out_specs,
        out_shape=out_shape,
        scratch_shapes=[pltpu.VMEM((tm, width), F32)],
        compiler_params=_cparams("arbitrary"),
        name="gmlp",
    )(h, w, w, lnw.reshape(1, width), lnb.reshape(1, width), ws_t, bs_t)
    return res


def _merge_kernel(h_ref, a_ref, g_ref, wa_ref, wb_ref, wba_ref, wbg_ref, o_ref):
    h = h_ref[...]
    ga = jax.nn.sigmoid(_dot(h, wa_ref[...]))
    gb = jax.nn.sigmoid(_dot(h, wb_ref[...]))
    o_ref[...] = (ga * _dot(a_ref[...], wba_ref[...]) + gb * _dot(g_ref[...], wbg_ref[...])).astype(BF16)


def _merge(h, attn, gm, w_in, col_a, col_b, wba, wbg, tm, tn):
    t, k = h.shape
    n = wba.shape[1]
    lhs = pl.BlockSpec((tm, k), lambda j, i: (i, 0))
    return pl.pallas_call(
        _merge_kernel,
        grid=(n // tn, t // tm),
        in_specs=[lhs, lhs, lhs,
                  pl.BlockSpec((k, tn), lambda j, i: (0, j + col_a // tn)),
                  pl.BlockSpec((k, tn), lambda j, i: (0, j + col_b // tn)),
                  pl.BlockSpec((k, tn), lambda j, i: (0, j)),
                  pl.BlockSpec((k, tn), lambda j, i: (0, j))],
        out_specs=pl.BlockSpec((tm, tn), lambda j, i: (i, j)),
        out_shape=jax.ShapeDtypeStruct((t, n), BF16),
        compiler_params=_cparams("arbitrary", "arbitrary"),
        name="merge",
    )(h, attn, gm, w_in, w_in, wba, wbg)


def _out_proj_kernel(m_ref, w_ref, x_ref, npost_ref, npre_ref, x1_ref, h2_ref):
    y = _dot(m_ref[...], w_ref[...])
    x1 = x_ref[...] + _rms(y, npost_ref[...], NORM_EPS)
    x1_ref[...] = x1
    h2_ref[...] = _rms(x1, npre_ref[...], NORM_EPS).astype(BF16)


def _out_proj(merged, w_out, x, npost, npre, tm):
    t, k = merged.shape
    d = w_out.shape[1]
    row = pl.BlockSpec((tm, d), lambda i: (i, 0))
    vec = pl.BlockSpec((1, d), lambda i: (0, 0))
    return pl.pallas_call(
        _out_proj_kernel,
        grid=(t // tm,),
        in_specs=[pl.BlockSpec((tm, k), lambda i: (i, 0)),
                  pl.BlockSpec((k, d), lambda i: (0, 0), pipeline_mode=pl.Buffered(1)),
                  row, vec, vec],
        out_specs=[row, row],
        out_shape=[jax.ShapeDtypeStruct((t, d), F32), jax.ShapeDtypeStruct((t, d), BF16)],
        compiler_params=_cparams("arbitrary"),
        name="out_proj",
    )(merged, w_out, x, npost.reshape(1, d), npre.reshape(1, d))


def _ffn_hidden_kernel(h_ref, wg_ref, wu_ref, o_ref):
    h = h_ref[...]
    o_ref[...] = (jax.nn.silu(_dot(h, wg_ref[...])) * _dot(h, wu_ref[...])).astype(BF16)


def _ffn_hidden(h2, wg, wu, tm, tn):
    t, k = h2.shape
    f = wg.shape[1]
    wspec = pl.BlockSpec((k, tn), lambda j, i: (0, j))
    return pl.pallas_call(
        _ffn_hidden_kernel,
        grid=(f // tn, t // tm),
        in_specs=[pl.BlockSpec((tm, k), lambda j, i: (i, 0)), wspec, wspec],
        out_specs=pl.BlockSpec((tm, tn), lambda j, i: (i, j)),
        out_shape=jax.ShapeDtypeStruct((t, f), BF16),
        compiler_params=_cparams("arbitrary", "arbitrary"),
        name="ffn_hidden",
    )(h2, wg, wu)


def _ffn_down_kernel(a_ref, w_ref, x1_ref, n_ref, o_ref, acc_ref):
    kk = pl.program_id(1)

    @pl.when(kk == 0)
    def _():
        acc_ref[...] = _dot(a_ref[...], w_ref[...])

    @pl.when(kk > 0)
    def _():
        acc_ref[...] += _dot(a_ref[...], w_ref[...])

    @pl.when(kk == pl.num_programs(1) - 1)
    def _():
        o_ref[...] = x1_ref[...] + _rms(acc_ref[...], n_ref[...], NORM_EPS)


def _ffn_down(hidden, wd, x1, npost, tm, tk):
    t, f = hidden.shape
    d = wd.shape[1]
    row = pl.BlockSpec((tm, d), lambda i, kk: (i, 0))
    return pl.pallas_call(
        _ffn_down_kernel,
        grid=(t // tm, f // tk),
        in_specs=[pl.BlockSpec((tm, tk), lambda i, kk: (i, kk)),
                  pl.BlockSpec((tk, d), lambda i, kk: (kk, 0)),
                  row,
                  pl.BlockSpec((1, d), lambda i, kk: (0, 0))],
        out_specs=row,
        out_shape=jax.ShapeDtypeStruct((t, d), F32),
        scratch_shapes=[pltpu.VMEM((tm, d), F32)],
        compiler_params=_cparams("arbitrary", "arbitrary"),
        name="ffn_down",
    )(hidden, wd, x1, npost.reshape(1, d))


def _largest_tile(t, cap):
    tm = min(t, cap)
    while t % tm:
        tm //= 2
    return tm


def _layer(x, p, lambda_init, cache):
    b, s, d = x.shape
    t = b * s
    n_heads = d // V_DIM
    x2 = x.reshape(t, d)
    tm_big = _largest_tile(t, 1024)
    tm_mid = _largest_tile(t, 512)

    h = _norm_cast(x2, p["norm_mix_pre"], tm_mid)

    qk_w = n_heads * 2 * HEAD_DIM
    col_k, col_v = qk_w, 2 * qk_w
    col_u = col_v + n_heads * V_DIM
    col_g = col_u + d
    col_a = col_g + d
    col_b = col_a + d
    w_in = p["w_in"]
    tn = 1024
    (q,) = _proj(h, w_in, 0, qk_w, tm_big, tn, scale=np.float32(HEAD_DIM ** -0.5 * LOG2E))
    k_f32, k_bf = _proj(h, w_in, col_k, qk_w, tm_big, tn, want_f32=True)
    v_f32, v_bf = _proj(h, w_in, col_v, n_heads * V_DIM, tm_big, tn, want_f32=True)

    slopes = 2.0 ** (-8.0 * jnp.arange(1, n_heads + 1, dtype=F32) / n_heads)
    lam_vecs = jnp.stack([p["lambda_q1"], p["lambda_k1"], p["lambda_q2"], p["lambda_k2"]])
    q3, k3, v3 = (a.reshape(b, s, -1) for a in (q, k_bf, v_bf))
    if cache is None:
        attn = _attn_prompt(q3, k3, v3, slopes, lam_vecs, p["subln_w"], n_heads, lambda_init,
                            _largest_tile(s, 512))
        rows_per_stream = GM_CHUNK
        ws_t = p["gm_ws"]
        bs_t = p["gm_bs"]
    else:
        ck, cv = cache
        past = ck.shape[1]
        attn = _attn_sample(q3, k3, v3, ck.reshape(b, past, -1), cv.reshape(b, past, -1),
                            slopes, lam_vecs, p["subln_w"], n_heads, lambda_init)
        rows_per_stream = s
        reps = GM_CHUNK // s
        ws_t = jnp.tile(p["gm_ws"][:, :s, :s], (1, reps, reps))
        bs_t = jnp.tile(p["gm_bs"][:, :s], (1, reps))
    attn = attn.reshape(t, -1)

    res = _gmlp(h, w_in, col_u, col_g, p["gm_ln_w"], p["gm_ln_b"], ws_t,
                bs_t.reshape(GM_GROUPS, GM_CHUNK, 1), _largest_tile(t, 256), rows_per_stream,
                want_vn=cache is not None)
    gm = res[0]
    vn = res[1] if cache is not None else None

    merged = _merge(h, attn, gm, w_in, col_a, col_b, p["w_branch_attn"], p["w_branch_gmlp"],
                    tm_big, 512)
    x1, h2 = _out_proj(merged, p["w_out"], x2, p["norm_mix_post"], p["norm_ffn_pre"], tm_mid)
    hidden = _ffn_hidden(h2, p["w_ffn_gate"], p["w_ffn_up"], tm_big, 512)
    y = _ffn_down(hidden, p["w_ffn_down"], x1, p["norm_ffn_post"], tm_mid,
                  p["w_ffn_down"].shape[0] // 4)
    return y.reshape(b, s, d), k_f32, v_f32, vn


def kernel(x_prompt, x_sample, cache_k, cache_v, norm_mix_pre, norm_mix_post, w_in, lambda_q1, lambda_k1, lambda_q2, lambda_k2, subln_w, gm_ln_w, gm_ln_b, gm_ws, gm_bs, w_branch_attn, w_branch_gmlp, w_out, norm_ffn_pre, norm_ffn_post, w_ffn_gate, w_ffn_up, w_ffn_down):
    depth = w_in.shape[0]
    bp, sp, d = x_prompt.shape
    bs_, ss, _ = x_sample.shape
    n_heads = d // V_DIM
    y_p, y_s = x_prompt, x_sample
    kp, vp, ks, vs, gs = [], [], [], [], []
    for l in range(depth):
        p = dict(norm_mix_pre=norm_mix_pre[l], norm_mix_post=norm_mix_post[l],
                 w_in=w_in[l].astype(BF16),
                 lambda_q1=lambda_q1[l], lambda_k1=lambda_k1[l], lambda_q2=lambda_q2[l],
                 lambda_k2=lambda_k2[l], subln_w=subln_w[l], gm_ln_w=gm_ln_w[l], gm_ln_b=gm_ln_b[l],
                 gm_ws=gm_ws[l], gm_bs=gm_bs[l],
                 w_branch_attn=w_branch_attn[l].astype(BF16),
                 w_branch_gmlp=w_branch_gmlp[l].astype(BF16),
                 w_out=w_out[l].astype(BF16),
                 norm_ffn_pre=norm_ffn_pre[l], norm_ffn_post=norm_ffn_post[l],
                 w_ffn_gate=w_ffn_gate[l].astype(BF16), w_ffn_up=w_ffn_up[l].astype(BF16),
                 w_ffn_down=w_ffn_down[l].astype(BF16))
        lambda_init = 0.8 - 0.6 * math.exp(-0.3 * l)
        y_p, k1, v1, _ = _layer(y_p, p, lambda_init, None)
        y_s, k2, v2, g2 = _layer(y_s, p, lambda_init, (cache_k[l], cache_v[l]))
        kp.append(k1.reshape(bp, sp, n_heads, 2, HEAD_DIM))
        vp.append(v1.reshape(bp, sp, n_heads, V_DIM))
        ks.append(k2.reshape(bs_, ss, n_heads, 2, HEAD_DIM))
        vs.append(v2.reshape(bs_, ss, n_heads, V_DIM))
        gs.append(g2.reshape(bs_, ss, GM_GROUPS, d // GM_GROUPS))
    return (y_p, y_s, jnp.stack(kp), jnp.stack(vp), jnp.stack(ks), jnp.stack(vs), jnp.stack(gs))
```

```python
import functools
import math

import jax
import jax.numpy as jnp
import numpy as np
from jax import lax
from jax.experimental import pallas as pl
from jax.experimental.pallas import tpu as pltpu

CHUNK = 64
HEAD_DIM = 128
V_DIM = 2 * HEAD_DIM
GM_CHUNK = 128
GM_GROUPS = 8
NORM_EPS = 1e-6
SUBLN_EPS = 1e-5
LOG2E = math.log2(math.e)
NEG_BIG = -1e30

V7X_VMEM_BYTES = 64 * 1024 * 1024
VMEM_LIMIT_BYTES = 56 * 1024 * 1024

BF16 = jnp.bfloat16
F32 = jnp.float32


def _cparams(*sem, flags=None):
    return pltpu.CompilerParams(dimension_semantics=sem, vmem_limit_bytes=VMEM_LIMIT_BYTES,
                                flags=flags)


def _rms(xf, w, eps):
    return xf * lax.rsqrt(jnp.mean(xf * xf, axis=-1, keepdims=True) + eps) * w


def _gelu(x):
    return 0.5 * x * (1.0 + lax.erf(x * np.float32(np.sqrt(0.5))))


def _dot(a, b):
    return jnp.dot(a, b, preferred_element_type=F32)


def _dot_nt(a, b):
    return lax.dot_general(a, b, (((1,), (1,)), ((), ())), preferred_element_type=F32)


def _norm_cast_kernel(x_ref, w_ref, o_ref):
    o_ref[...] = _rms(x_ref[...], w_ref[...], NORM_EPS).astype(BF16)


def _norm_cast(x, w, tm):
    t, d = x.shape
    return pl.pallas_call(
        _norm_cast_kernel,
        grid=(t // tm,),
        in_specs=[pl.BlockSpec((tm, d), lambda i: (i, 0)),
                  pl.BlockSpec((1, d), lambda i: (0, 0))],
        out_specs=pl.BlockSpec((tm, d), lambda i: (i, 0)),
        out_shape=jax.ShapeDtypeStruct((t, d), BF16),
        compiler_params=_cparams("arbitrary"),
        name="norm_cast",
    )(x, w.reshape(1, d))


def _proj_kernel(h_ref, w_ref, *o_refs, scale, want_f32, want_bf16):
    z = _dot(h_ref[...], w_ref[...])
    if scale is not None:
        z = z * scale
    i = 0
    if want_f32:
        o_refs[i][...] = z
        i += 1
    if want_bf16:
        o_refs[i][...] = z.astype(BF16)


def _proj(h, w, col0, ncols, tm, tn, *, scale=None, want_f32=False, want_bf16=True):
    t, k = h.shape
    off = col0 // tn
    out_shape, out_specs = [], []
    if want_f32:
        out_shape.append(jax.ShapeDtypeStruct((t, ncols), F32))
        out_specs.append(pl.BlockSpec((tm, tn), lambda n, m: (m, n)))
    if want_bf16:
        out_shape.append(jax.ShapeDtypeStruct((t, ncols), BF16))
        out_specs.append(pl.BlockSpec((tm, tn), lambda n, m: (m, n)))
    return pl.pallas_call(
        functools.partial(_proj_kernel, scale=scale, want_f32=want_f32, want_bf16=want_bf16),
        grid=(ncols // tn, t // tm),
        in_specs=[pl.BlockSpec((tm, k), lambda n, m: (m, 0)),
                  pl.BlockSpec((k, tn), lambda n, m: (0, n + off))],
        out_specs=out_specs,
        out_shape=out_shape,
        compiler_params=_cparams("arbitrary", "arbitrary"),
        name="in_proj",
    )(h, w)


def _lambda_full(lam_ref, lambda_init):
    lv = lam_ref[...]
    s1 = jnp.sum(lv[0:1] * lv[1:2], axis=-1, keepdims=True)
    s2 = jnp.sum(lv[2:3] * lv[3:4], axis=-1, keepdims=True)
    return jnp.exp(s1) - jnp.exp(s2) + lambda_init


ONES_ROWS = 16
AUX_SPLIT = 256


def _bf16_parts(x):
    a = x.astype(BF16).astype(F32)
    b = (x - a).astype(BF16).astype(F32)
    c = (x - a - b).astype(BF16).astype(F32)
    return a, b, c


def _alibi_aux(n, slope2, pos_side):
    col = lax.broadcasted_iota(jnp.int32, (n, HEAD_DIM), 1)
    pos = lax.broadcasted_iota(jnp.int32, (n, HEAD_DIM), 0)
    lo = jnp.bitwise_and(pos, AUX_SPLIT - 1)
    hi = pos - lo
    a, b, c = _bf16_parts(jnp.full((n, HEAD_DIM), slope2, F32))
    k3 = col - 3 * ((col >= 3).astype(jnp.int32) + (col >= 6).astype(jnp.int32)
                    + (col >= 9).astype(jnp.int32))
    slope_piece = jnp.where(k3 == 0, a, jnp.where(k3 == 1, b, c))
    first_half = (col < 3) | ((col >= 6) & (col < 9))
    pos_piece = jnp.where(first_half, hi, lo).astype(F32) * pos_side
    pos_cols = (col < 6) if pos_side > 0 else ((col >= 6) & (col < 12))
    slope_cols = ((col >= 6) & (col < 12)) if pos_side > 0 else (col < 6)
    return jnp.where(pos_cols, pos_piece, jnp.where(slope_cols, slope_piece, 0.0)).astype(BF16)


def _attn_prompt_kernel(slopes_ref, qi_tab, j_tab, last_tab, q_ref, k_ref, vt_ref, lam_ref, sw_ref,
                        o_ref, corr_ref, auxk_ref, auxq_ref, s_a, s_b, mx_a, mx_b, p_ref, m_ref,
                        acc_ref, *, tq, tk, nsteps, lambda_init):
    h = pl.program_id(1)
    slope2 = slopes_ref[h] * LOG2E

    c = lax.broadcasted_iota(jnp.int32, (tq, tq), 0)
    r = lax.broadcasted_iota(jnp.int32, (tq, tq), 1)
    allowed = jnp.right_shift(c, 6) <= jnp.right_shift(r, 6)
    square = jnp.where(allowed, jnp.maximum(c - r, 0).astype(F32) * (-2.0 * slope2), NEG_BIG)
    corr_ref[0, :tq, :] = square
    corr_ref[0, tq:, :] = jnp.full((tk - tq, tq), NEG_BIG, F32)
    corr_ref[1, :tk - tq, :] = jnp.zeros((tk - tq, tq), F32)
    corr_ref[1, tk - tq:, :] = square
    auxk_ref[...] = _alibi_aux(tk, slope2, 1)
    auxq_ref[...] = _alibi_aux(tq, slope2, -1)
    acc_ref[...] = jnp.zeros_like(acc_ref)
    m_ref[...] = jnp.full_like(m_ref, NEG_BIG)

    def q_rows(blk):
        return pl.ds(pl.multiple_of(blk * tq, tq), tq)

    def k_rows(blk):
        return pl.ds(pl.multiple_of(blk * tk, tk), tk)

    def scores(t, is_last, s_out, mx_out):
        blk_q, blk_k = qi_tab[t], j_tab[t]
        for mp in range(2):
            cols = slice(mp * HEAD_DIM, (mp + 1) * HEAD_DIM)
            kk = jnp.concatenate([k_ref[0, k_rows(blk_k), cols], auxk_ref[...]], axis=1)
            qq = jnp.concatenate([q_ref[0, q_rows(blk_q), cols], auxq_ref[...]], axis=1)
            s = _dot_nt(kk, qq)
            if is_last:
                s = s + corr_ref[jnp.bitwise_and(blk_q, 1)]
            s_out[mp] = s
            mx_out[mp] = jnp.max(s, axis=0, keepdims=True)

    def consume(t, s_in, mx_in):
        blk_q, blk_k = qi_tab[t], j_tab[t]
        cj = slope2 * (blk_q * tq - blk_k * tk).astype(F32)
        vj = vt_ref[0, 0, :, k_rows(blk_k)]
        for mp in range(2):
            m_old = jnp.where(blk_k == 0, NEG_BIG, m_ref[mp])
            m_new = jnp.maximum(m_old, mx_in[mp] - cj)
            m_ref[mp] = m_new
            p_ref[mp] = jnp.exp2(s_in[mp] - (m_new + cj)).astype(BF16)
            acc_ref[mp] = jnp.exp2(m_old - m_new) * acc_ref[mp] + _dot(vj, p_ref[mp])

    def finish(blk_q):
        lam = _lambda_full(lam_ref, lambda_init)
        o1 = acc_ref[0, :V_DIM, :] * (1.0 / acc_ref[0, V_DIM:V_DIM + 1, :])
        o2 = acc_ref[1, :V_DIM, :] * (1.0 / acc_ref[1, V_DIM:V_DIM + 1, :])
        o = o1 - lam * o2
        on = o * lax.rsqrt(jnp.mean(o * o, axis=0, keepdims=True) + SUBLN_EPS)
        o_ref[0, q_rows(blk_q), :] = (on.T * sw_ref[...] * (1.0 - lambda_init)).astype(BF16)

    def step(t, s_cur, mx_cur, s_nxt, mx_nxt):
        for nxt_last in (0, 1):
            @pl.when(last_tab[t + 1] == nxt_last)
            def _():
                scores(t + 1, bool(nxt_last), s_nxt, mx_nxt)
                consume(t, s_cur, mx_cur)

        @pl.when(last_tab[t] == 1)
        def _():
            finish(qi_tab[t])

    scores(0, True, s_a, mx_a)

    def pair(tt, carry):
        step(2 * tt, s_a, mx_a, s_b, mx_b)
        step(2 * tt + 1, s_b, mx_b, s_a, mx_a)
        return carry

    lax.fori_loop(0, nsteps // 2, pair, 0)
    if nsteps % 2:
        step(nsteps - 1, s_a, mx_a, s_b, mx_b)


def _attn_prompt(q, k, v, slopes, lam_vecs, subln_w, n_heads, lambda_init, tq):
    b, s, _ = q.shape
    tk = 2 * tq
    assert s % tk == 0
    nq = s // tq
    vt = jnp.swapaxes(v, 1, 2).reshape(b, n_heads, V_DIM, s)
    vt = jnp.concatenate([vt, jnp.ones((b, n_heads, ONES_ROWS, s), BF16)], axis=2)
    pairs = [(qi, j, int(j == qi // 2)) for qi in range(nq) for j in range(qi // 2 + 1)]
    pairs.append(pairs[-1])
    qi_tab, j_tab, last_tab = (jnp.asarray(np.array([p[i] for p in pairs], np.int32))
                               for i in range(3))
    nsteps = len(pairs) - 1
    vrows = V_DIM + ONES_ROWS
    seq_spec = pl.BlockSpec((1, s, V_DIM), lambda bi, hi, *_: (bi, 0, hi))
    grid_spec = pltpu.PrefetchScalarGridSpec(
        num_scalar_prefetch=4,
        grid=(b, n_heads),
        in_specs=[seq_spec, seq_spec,
                  pl.BlockSpec((1, 1, vrows, s), lambda bi, hi, *_: (bi, hi, 0, 0)),
                  pl.BlockSpec((4, HEAD_DIM), lambda bi, hi, *_: (0, 0)),
                  pl.BlockSpec((1, V_DIM), lambda bi, hi, *_: (0, 0))],
        out_specs=seq_spec,
        scratch_shapes=[pltpu.VMEM((2, tk, tq), F32),
                        pltpu.VMEM((tk, HEAD_DIM), BF16),
                        pltpu.VMEM((tq, HEAD_DIM), BF16),
                        pltpu.VMEM((2, tk, tq), F32),
                        pltpu.VMEM((2, tk, tq), F32),
                        pltpu.VMEM((2, 1, tq), F32),
                        pltpu.VMEM((2, 1, tq), F32),
                        pltpu.VMEM((2, tk, tq), BF16),
                        pltpu.VMEM((2, 1, tq), F32),
                        pltpu.VMEM((2, vrows, tq), F32)],
    )
    return pl.pallas_call(
        functools.partial(_attn_prompt_kernel, tq=tq, tk=tk, nsteps=nsteps, lambda_init=lambda_init),
        grid_spec=grid_spec,
        out_shape=jax.ShapeDtypeStruct(q.shape, BF16),
        compiler_params=_cparams("arbitrary", "arbitrary"),
        name="attn_prompt",
    )(slopes, qi_tab, j_tab, last_tab, q, k, vt, lam_vecs, subln_w.reshape(1, V_DIM))


def _attn_sample_kernel(slopes_ref, q_ref, kn_ref, vn_ref, kc_ref, vc_ref, lam_ref, sw_ref, o_ref,
                        *, past, n, lambda_init):
    h = pl.program_id(1)
    slope2 = slopes_ref[h] * LOG2E
    r_c = lax.broadcasted_iota(jnp.int32, (n, past), 0)
    c_c = lax.broadcasted_iota(jnp.int32, (n, past), 1)
    bias_c = (past + r_c - c_c).astype(F32) * (-slope2)
    r_n = lax.broadcasted_iota(jnp.int32, (n, n), 0)
    c_n = lax.broadcasted_iota(jnp.int32, (n, n), 1)
    bias_n = jnp.abs(r_n - c_n).astype(F32) * (-slope2)
    vc = vc_ref[0].astype(BF16)
    vn = vn_ref[0]
    outs = []
    for mp in range(2):
        sl = slice(mp * HEAD_DIM, (mp + 1) * HEAD_DIM)
        qm = q_ref[0, :, sl]
        s_c = _dot_nt(qm, kc_ref[0, :, sl].astype(BF16)) + bias_c
        s_n = _dot_nt(qm, kn_ref[0, :, sl]) + bias_n
        m = jnp.maximum(jnp.max(s_c, axis=-1, keepdims=True), jnp.max(s_n, axis=-1, keepdims=True))
        p_c = jnp.exp2(s_c - m)
        p_n = jnp.exp2(s_n - m)
        l = jnp.sum(p_c, axis=-1, keepdims=True) + jnp.sum(p_n, axis=-1, keepdims=True)
        acc = _dot(p_c.astype(BF16), vc) + _dot(p_n.astype(BF16), vn)
        outs.append(acc / l)
    lam = _lambda_full(lam_ref, lambda_init)
    o = outs[0] - lam * outs[1]
    o_ref[0] = (_rms(o, sw_ref[...], SUBLN_EPS) * (1.0 - lambda_init)).astype(BF16)


def _attn_sample(q, kn, vn, kc, vc, layer, slopes, lam_vecs, subln_w, n_heads, lambda_init):
    b, n, _ = q.shape
    past = kc.shape[1]
    new_spec = pl.BlockSpec((1, n, V_DIM), lambda bi, hi, sl: (bi, 0, hi))
    cache_spec = pl.BlockSpec((1, past, V_DIM), lambda bi, hi, sl: (layer * b + bi, 0, hi))
    grid_spec = pltpu.PrefetchScalarGridSpec(
        num_scalar_prefetch=1,
        grid=(b, n_heads),
        in_specs=[new_spec, new_spec, new_spec, cache_spec, cache_spec,
                  pl.BlockSpec((4, HEAD_DIM), lambda bi, hi, sl: (0, 0)),
                  pl.BlockSpec((1, V_DIM), lambda bi, hi, sl: (0, 0))],
        out_specs=new_spec,
    )
    return pl.pallas_call(
        functools.partial(_attn_sample_kernel, past=past, n=n, lambda_init=lambda_init),
        grid_spec=grid_spec,
        out_shape=jax.ShapeDtypeStruct(q.shape, BF16),
        compiler_params=_cparams("arbitrary", "arbitrary"),
        name="attn_sample",
    )(slopes, q, kn, vn, kc, vc, lam_vecs, subln_w.reshape(1, V_DIM))


def _gmlp_kernel(h_ref, wu_ref, wg_ref, lnw_ref, lnb_ref, ws_ref, bs_ref, *refs,
                 tm, width, rows_per_stream, want_vn):
    if want_vn:
        o_ref, vn_out_ref, g_ref = refs
    else:
        o_ref, g_ref = refs
    cb = 512
    gd = width // GM_GROUPS
    h = h_ref[...]
    rowsum = jnp.zeros((tm, 1), F32)
    for j in range(width // cb):
        g = _gelu(_dot(h, wg_ref[:, j * cb:(j + 1) * cb]))
        g_ref[:, j * cb:(j + 1) * cb] = g
        rowsum = rowsum + jnp.sum(g, axis=-1, keepdims=True)
    mean = rowsum / width
    sq = jnp.zeros((tm, 1), F32)
    for j in range(width // cb):
        xc = g_ref[:, j * cb:(j + 1) * cb] - mean
        sq = sq + jnp.sum(xc * xc, axis=-1, keepdims=True)
    rstd = lax.rsqrt(sq / width + NORM_EPS)

    r = lax.broadcasted_iota(jnp.int32, (GM_CHUNK, GM_CHUNK), 0)
    c = lax.broadcasted_iota(jnp.int32, (GM_CHUNK, GM_CHUNK), 1)
    causal = (c <= r) & ((c // rows_per_stream) == (r // rows_per_stream))
    for j in range(width // cb):
        cols = slice(j * cb, (j + 1) * cb)
        vn = (g_ref[:, cols] - mean) * rstd * lnw_ref[:, cols] + lnb_ref[:, cols]
        if want_vn:
            vn_out_ref[:, cols] = vn
        vnb = vn.astype(BF16)
        u = _gelu(_dot(h, wu_ref[:, cols]))
        for gg in range(cb // gd):
            grp = j * (cb // gd) + gg
            wsm = jnp.where(causal, ws_ref[grp], 0.0).astype(BF16)
            bias = bs_ref[grp]
            for ch in range(tm // GM_CHUNK):
                rows = slice(ch * GM_CHUNK, (ch + 1) * GM_CHUNK)
                mixed = _dot(wsm, vnb[rows, gg * gd:(gg + 1) * gd]) + bias
                o_ref[rows, j * cb + gg * gd:j * cb + (gg + 1) * gd] = (
                    u[rows, gg * gd:(gg + 1) * gd] * mixed).astype(BF16)


def _gmlp(h, w, col_u, col_g, lnw, lnb, ws_t, bs_t, tm, rows_per_stream, want_vn):
    t, k = h.shape
    width = lnw.shape[-1]
    out_shape = [jax.ShapeDtypeStruct((t, width), BF16)]
    out_specs = [pl.BlockSpec((tm, width), lambda i: (i, 0))]
    if want_vn:
        out_shape.append(jax.ShapeDtypeStruct((t, width), F32))
        out_specs.append(pl.BlockSpec((tm, width), lambda i: (i, 0)))
    one = pl.Buffered(1)
    res = pl.pallas_call(
        functools.partial(_gmlp_kernel, tm=tm, width=width, rows_per_stream=rows_per_stream,
                          want_vn=want_vn),
        grid=(t // tm,),
        in_specs=[pl.BlockSpec((tm, k), lambda i: (i, 0)),
                  pl.BlockSpec((k, width), lambda i: (0, col_u // width), pipeline_mode=one),
                  pl.BlockSpec((k, width), lambda i: (0, col_g // width), pipeline_mode=one),
                  pl.BlockSpec((1, width), lambda i: (0, 0)),
                  pl.BlockSpec((1, width), lambda i: (0, 0)),
                  pl.BlockSpec((GM_GROUPS, GM_CHUNK, GM_CHUNK), lambda i: (0, 0, 0)),
                  pl.BlockSpec((GM_GROUPS, GM_CHUNK, 1), lambda i: (0, 0, 0))],
        out_specs=out_specs,
        out_shape=out_shape,
        scratch_shapes=[pltpu.VMEM((tm, width), F32)],
        compiler_params=_cparams("arbitrary"),
        name="gmlp",
    )(h, w, w, lnw.reshape(1, width), lnb.reshape(1, width), ws_t, bs_t)
    return res


def _merge_kernel(h_ref, a_ref, g_ref, wa_ref, wb_ref, wba_ref, wbg_ref, o_ref):
    h = h_ref[...]
    ga = jax.nn.sigmoid(_dot(h, wa_ref[...]))
    gb = jax.nn.sigmoid(_dot(h, wb_ref[...]))
    o_ref[...] = (ga * _dot(a_ref[...], wba_ref[...]) + gb * _dot(g_ref[...], wbg_ref[...])).astype(BF16)


def _merge(h, attn, gm, w_in, col_a, col_b, wba, wbg, tm, tn):
    t, k = h.shape
    n = wba.shape[1]
    lhs = pl.BlockSpec((tm, k), lambda j, i: (i, 0))
    return pl.pallas_call(
        _merge_kernel,
        grid=(n // tn, t // tm),
        in_specs=[lhs, lhs, lhs,
                  pl.BlockSpec((k, tn), lambda j, i: (0, j + col_a // tn)),
                  pl.BlockSpec((k, tn), lambda j, i: (0, j + col_b // tn)),
                  pl.BlockSpec((k, tn), lambda j, i: (0, j)),
                  pl.BlockSpec((k, tn), lambda j, i: (0, j))],
        out_specs=pl.BlockSpec((tm, tn), lambda j, i: (i, j)),
        out_shape=jax.ShapeDtypeStruct((t, n), BF16),
        compiler_params=_cparams("arbitrary", "arbitrary"),
        name="merge",
    )(h, attn, gm, w_in, w_in, wba, wbg)


def _out_proj_kernel(m_ref, w_ref, x_ref, npost_ref, npre_ref, x1_ref, h2_ref):
    y = _dot(m_ref[...], w_ref[...])
    x1 = x_ref[...] + _rms(y, npost_ref[...], NORM_EPS)
    x1_ref[...] = x1
    h2_ref[...] = _rms(x1, npre_ref[...], NORM_EPS).astype(BF16)


def _out_proj(merged, w_out, x, npost, npre, tm):
    t, k = merged.shape
    d = w_out.shape[1]
    row = pl.BlockSpec((tm, d), lambda i: (i, 0))
    vec = pl.BlockSpec((1, d), lambda i: (0, 0))
    return pl.pallas_call(
        _out_proj_kernel,
        grid=(t // tm,),
        in_specs=[pl.BlockSpec((tm, k), lambda i: (i, 0)),
                  pl.BlockSpec((k, d), lambda i: (0, 0), pipeline_mode=pl.Buffered(1)),
                  row, vec, vec],
        out_specs=[row, row],
        out_shape=[jax.ShapeDtypeStruct((t, d), F32), jax.ShapeDtypeStruct((t, d), BF16)],
        compiler_params=_cparams("arbitrary"),
        name="out_proj",
    )(merged, w_out, x, npost.reshape(1, d), npre.reshape(1, d))


def _ffn_hidden_kernel(h_ref, wg_ref, wu_ref, o_ref):
    h = h_ref[...]
    o_ref[...] = (jax.nn.silu(_dot(h, wg_ref[...])) * _dot(h, wu_ref[...])).astype(BF16)


def _ffn_hidden(h2, wg, wu, tm, tn):
    t, k = h2.shape
    f = wg.shape[1]
    wspec = pl.BlockSpec((k, tn), lambda j, i: (0, j))
    return pl.pallas_call(
        _ffn_hidden_kernel,
        grid=(f // tn, t // tm),
        in_specs=[pl.BlockSpec((tm, k), lambda j, i: (i, 0)), wspec, wspec],
        out_specs=pl.BlockSpec((tm, tn), lambda j, i: (i, j)),
        out_shape=jax.ShapeDtypeStruct((t, f), BF16),
        compiler_params=_cparams("arbitrary", "arbitrary"),
        name="ffn_hidden",
    )(h2, wg, wu)


def _ffn_down_kernel(a_ref, w_ref, x1_ref, n_ref, o_ref, acc_ref):
    kk = pl.program_id(1)

    @pl.when(kk == 0)
    def _():
        acc_ref[...] = _dot(a_ref[...], w_ref[...])

    @pl.when(kk > 0)
    def _():
        acc_ref[...] += _dot(a_ref[...], w_ref[...])

    @pl.when(kk == pl.num_programs(1) - 1)
    def _():
        o_ref[...] = x1_ref[...] + _rms(acc_ref[...], n_ref[...], NORM_EPS)


def _ffn_down(hidden, wd, x1, npost, tm, tk):
    t, f = hidden.shape
    d = wd.shape[1]
    row = pl.BlockSpec((tm, d), lambda i, kk: (i, 0))
    return pl.pallas_call(
        _ffn_down_kernel,
        grid=(t // tm, f // tk),
        in_specs=[pl.BlockSpec((tm, tk), lambda i, kk: (i, kk)),
                  pl.BlockSpec((tk, d), lambda i, kk: (kk, 0)),
                  row,
                  pl.BlockSpec((1, d), lambda i, kk: (0, 0))],
        out_specs=row,
        out_shape=jax.ShapeDtypeStruct((t, d), F32),
        scratch_shapes=[pltpu.VMEM((tm, d), F32)],
        compiler_params=_cparams("arbitrary", "arbitrary"),
        name="ffn_down",
    )(hidden, wd, x1, npost.reshape(1, d))


def _largest_tile(t, cap):
    tm = min(t, cap)
    while t % tm:
        tm //= 2
    return tm


def _layer(x, p, lambda_init, cache):
    b, s, d = x.shape
    t = b * s
    n_heads = d // V_DIM
    x2 = x.reshape(t, d)
    tm_big = _largest_tile(t, 1024)
    tm_mid = _largest_tile(t, 512)

    h = _norm_cast(x2, p["norm_mix_pre"], tm_mid)

    qk_w = n_heads * 2 * HEAD_DIM
    col_k, col_v = qk_w, 2 * qk_w
    col_u = col_v + n_heads * V_DIM
    col_g = col_u + d
    col_a = col_g + d
    col_b = col_a + d
    w_in = p["w_in"]
    tn = 1024
    (q,) = _proj(h, w_in, 0, qk_w, tm_big, tn, scale=np.float32(HEAD_DIM ** -0.5 * LOG2E))
    k_f32, k_bf = _proj(h, w_in, col_k, qk_w, tm_big, tn, want_f32=True)
    v_f32, v_bf = _proj(h, w_in, col_v, n_heads * V_DIM, tm_big, tn, want_f32=True)

    slopes = 2.0 ** (-8.0 * jnp.arange(1, n_heads + 1, dtype=F32) / n_heads)
    lam_vecs = jnp.stack([p["lambda_q1"], p["lambda_k1"], p["lambda_q2"], p["lambda_k2"]])
    q3, k3, v3 = (a.reshape(b, s, -1) for a in (q, k_bf, v_bf))
    if cache is None:
        attn = _attn_prompt(q3, k3, v3, slopes, lam_vecs, p["subln_w"], n_heads, lambda_init,
                            _largest_tile(s, 512))
        rows_per_stream = GM_CHUNK
        ws_t = p["gm_ws"]
        bs_t = p["gm_bs"]
    else:
        ck, cv, layer = cache
        past = ck.shape[2]
        attn = _attn_sample(q3, k3, v3, ck.reshape(-1, past, d), cv.reshape(-1, past, d), layer,
                            slopes, lam_vecs, p["subln_w"], n_heads, lambda_init)
        rows_per_stream = s
        reps = GM_CHUNK // s
        ws_t = jnp.tile(p["gm_ws"][:, :s, :s], (1, reps, reps))
        bs_t = jnp.tile(p["gm_bs"][:, :s], (1, reps))
    attn = attn.reshape(t, -1)

    res = _gmlp(h, w_in, col_u, col_g, p["gm_ln_w"], p["gm_ln_b"], ws_t,
                bs_t.reshape(GM_GROUPS, GM_CHUNK, 1), _largest_tile(t, 256), rows_per_stream,
                want_vn=cache is not None)
    gm = res[0]
    vn = res[1] if cache is not None else None

    merged = _merge(h, attn, gm, w_in, col_a, col_b, p["w_branch_attn"], p["w_branch_gmlp"],
                    tm_big, 512)
    x1, h2 = _out_proj(merged, p["w_out"], x2, p["norm_mix_post"], p["norm_ffn_pre"], tm_mid)
    hidden = _ffn_hidden(h2, p["w_ffn_gate"], p["w_ffn_up"], tm_big, 512)
    y = _ffn_down(hidden, p["w_ffn_down"], x1, p["norm_ffn_post"], tm_mid,
                  p["w_ffn_down"].shape[0] // 4)
    return y.reshape(b, s, d), k_f32, v_f32, vn


def kernel(x_prompt, x_sample, cache_k, cache_v, norm_mix_pre, norm_mix_post, w_in, lambda_q1, lambda_k1, lambda_q2, lambda_k2, subln_w, gm_ln_w, gm_ln_b, gm_ws, gm_bs, w_branch_attn, w_branch_gmlp, w_out, norm_ffn_pre, norm_ffn_post, w_ffn_gate, w_ffn_up, w_ffn_down):
    depth = w_in.shape[0]
    bp, sp, d = x_prompt.shape
    bs_, ss, _ = x_sample.shape
    n_heads = d // V_DIM
    y_p, y_s = x_prompt, x_sample
    kp, vp, ks, vs, gs = [], [], [], [], []
    for l in range(depth):
        p = dict(norm_mix_pre=norm_mix_pre[l], norm_mix_post=norm_mix_post[l],
                 w_in=w_in[l].astype(BF16),
                 lambda_q1=lambda_q1[l], lambda_k1=lambda_k1[l], lambda_q2=lambda_q2[l],
                 lambda_k2=lambda_k2[l], subln_w=subln_w[l], gm_ln_w=gm_ln_w[l], gm_ln_b=gm_ln_b[l],
                 gm_ws=gm_ws[l], gm_bs=gm_bs[l],
                 w_branch_attn=w_branch_attn[l].astype(BF16),
                 w_branch_gmlp=w_branch_gmlp[l].astype(BF16),
                 w_out=w_out[l].astype(BF16),
                 norm_ffn_pre=norm_ffn_pre[l], norm_ffn_post=norm_ffn_post[l],
                 w_ffn_gate=w_ffn_gate[l].astype(BF16), w_ffn_up=w_ffn_up[l].astype(BF16),
                 w_ffn_down=w_ffn_down[l].astype(BF16))
        lambda_init = 0.8 - 0.6 * math.exp(-0.3 * l)
        y_p, k1, v1, _ = _layer(y_p, p, lambda_init, None)
        y_s, k2, v2, g2 = _layer(y_s, p, lambda_init, (cache_k, cache_v, l))
        kp.append(k1.reshape(bp, sp, n_heads, 2, HEAD_DIM))
        vp.append(v1.reshape(bp, sp, n_heads, V_DIM))
        ks.append(k2.reshape(bs_, ss, n_heads, 2, HEAD_DIM))
        vs.append(v2.reshape(bs_, ss, n_heads, V_DIM))
        gs.append(g2.reshape(bs_, ss, GM_GROUPS, d // GM_GROUPS))
    return (y_p, y_s, jnp.stack(kp), jnp.stack(vp), jnp.stack(ks), jnp.stack(vs), jnp.stack(gs))
```

```python
import functools
import math

import jax
import jax.numpy as jnp
import numpy as np
from jax import lax
from jax.experimental import pallas as pl
from jax.experimental.pallas import tpu as pltpu

CHUNK = 64
HEAD_DIM = 128
V_DIM = 2 * HEAD_DIM
GM_CHUNK = 128
GM_GROUPS = 8
NORM_EPS = 1e-6
SUBLN_EPS = 1e-5
LOG2E = math.log2(math.e)
NEG_BIG = -1e30

V7X_VMEM_BYTES = 64 * 1024 * 1024
VMEM_LIMIT_BYTES = 56 * 1024 * 1024

BF16 = jnp.bfloat16
F32 = jnp.float32


def _cparams(*sem, flags=None):
    return pltpu.CompilerParams(dimension_semantics=sem, vmem_limit_bytes=VMEM_LIMIT_BYTES,
                                flags=flags)


def _rms(xf, w, eps):
    return xf * lax.rsqrt(jnp.mean(xf * xf, axis=-1, keepdims=True) + eps) * w


def _gelu(x):
    return 0.5 * x * (1.0 + lax.erf(x * np.float32(np.sqrt(0.5))))


def _dot(a, b):
    return jnp.dot(a, b, preferred_element_type=F32)


def _dot_nt(a, b):
    return lax.dot_general(a, b, (((1,), (1,)), ((), ())), preferred_element_type=F32)


def _proj_kernel(h_ref, w_ref, *o_refs, scale, want_f32, want_bf16):
    z = _dot(h_ref[...], w_ref[...])
    if scale is not None:
        z = z * scale
    i = 0
    if want_f32:
        o_refs[i][...] = z
        i += 1
    if want_bf16:
        o_refs[i][...] = z.astype(BF16)


def _proj(h, w, col0, ncols, tm, tn, *, scale=None, want_f32=False, want_bf16=True):
    t, k = h.shape
    off = col0 // tn
    out_shape, out_specs = [], []
    if want_f32:
        out_shape.append(jax.ShapeDtypeStruct((t, ncols), F32))
        out_specs.append(pl.BlockSpec((tm, tn), lambda n, m: (m, n)))
    if want_bf16:
        out_shape.append(jax.ShapeDtypeStruct((t, ncols), BF16))
        out_specs.append(pl.BlockSpec((tm, tn), lambda n, m: (m, n)))
    return pl.pallas_call(
        functools.partial(_proj_kernel, scale=scale, want_f32=want_f32, want_bf16=want_bf16),
        grid=(ncols // tn, t // tm),
        in_specs=[pl.BlockSpec((tm, k), lambda n, m: (m, 0)),
                  pl.BlockSpec((k, tn), lambda n, m: (0, n + off))],
        out_specs=out_specs,
        out_shape=out_shape,
        compiler_params=_cparams("arbitrary", "arbitrary"),
        name="in_proj",
    )(h, w)


def _norm_proj_k_kernel(x_ref, nw_ref, w_ref, h_ref, kf_ref, kb_ref, *, tm, slabs):
    h = _rms(x_ref[...], nw_ref[...], NORM_EPS).astype(BF16)
    h_ref[...] = h
    z = _dot(h, w_ref[...])
    kb_ref[...] = z.astype(BF16)
    for c in range(slabs):
        kf_ref[pl.ds(c, tm, stride=slabs), :] = z[:, c * HEAD_DIM:(c + 1) * HEAD_DIM]


def _norm_proj_k(x, nw, w, col0, ncols, tm):
    t, d = x.shape
    slabs = ncols // HEAD_DIM
    row_d = pl.BlockSpec((tm, d), lambda i: (i, 0))
    row_n = pl.BlockSpec((tm, ncols), lambda i: (i, 0))
    return pl.pallas_call(
        functools.partial(_norm_proj_k_kernel, tm=tm, slabs=slabs),
        grid=(t // tm,),
        in_specs=[row_d,
                  pl.BlockSpec((1, d), lambda i: (0, 0)),
                  pl.BlockSpec((d, ncols), lambda i: (0, col0 // ncols),
                               pipeline_mode=pl.Buffered(1))],
        out_specs=[row_d, pl.BlockSpec((tm * slabs, HEAD_DIM), lambda i: (i, 0)), row_n],
        out_shape=[jax.ShapeDtypeStruct((t, d), BF16),
                   jax.ShapeDtypeStruct((t * slabs, HEAD_DIM), F32),
                   jax.ShapeDtypeStruct((t, ncols), BF16)],
        compiler_params=_cparams("arbitrary"),
        name="norm_proj_k",
    )(x, nw.reshape(1, d), w)


def _lambda_full(lam_ref, lambda_init):
    lv = lam_ref[...]
    s1 = jnp.sum(lv[0:1] * lv[1:2], axis=-1, keepdims=True)
    s2 = jnp.sum(lv[2:3] * lv[3:4], axis=-1, keepdims=True)
    return jnp.exp(s1) - jnp.exp(s2) + lambda_init


ONES_ROWS = 16
AUX_SPLIT = 256


def _bf16_parts(x):
    a = x.astype(BF16).astype(F32)
    b = (x - a).astype(BF16).astype(F32)
    c = (x - a - b).astype(BF16).astype(F32)
    return a, b, c


def _alibi_aux(n, slope2, pos_side):
    col = lax.broadcasted_iota(jnp.int32, (n, HEAD_DIM), 1)
    pos = lax.broadcasted_iota(jnp.int32, (n, HEAD_DIM), 0)
    lo = jnp.bitwise_and(pos, AUX_SPLIT - 1)
    hi = pos - lo
    a, b, c = _bf16_parts(jnp.full((n, HEAD_DIM), slope2, F32))
    k3 = col - 3 * ((col >= 3).astype(jnp.int32) + (col >= 6).astype(jnp.int32)
                    + (col >= 9).astype(jnp.int32))
    slope_piece = jnp.where(k3 == 0, a, jnp.where(k3 == 1, b, c))
    first_half = (col < 3) | ((col >= 6) & (col < 9))
    pos_piece = jnp.where(first_half, hi, lo).astype(F32) * pos_side
    pos_cols = (col < 6) if pos_side > 0 else ((col >= 6) & (col < 12))
    slope_cols = ((col >= 6) & (col < 12)) if pos_side > 0 else (col < 6)
    return jnp.where(pos_cols, pos_piece, jnp.where(slope_cols, slope_piece, 0.0)).astype(BF16)


KIND_FULL = 0
KIND_LAST_FULL = 1
KIND_LAST_HALF = 2


def _attn_prompt_kernel(slopes_ref, qi_tab, j_tab, kind_tab, q_ref, k_ref, vt_ref, lam_ref, sw_ref,
                        o_ref, corr_ref, auxk_ref, auxq_ref, s_a, s_b, mx_a, mx_b, p_ref, m_ref,
                        acc_ref, *, tq, tk, nsteps, lambda_init):
    h = pl.program_id(1)
    slope2 = slopes_ref[h] * LOG2E

    c = lax.broadcasted_iota(jnp.int32, (tq, tq), 0)
    r = lax.broadcasted_iota(jnp.int32, (tq, tq), 1)
    allowed = jnp.right_shift(c, 6) <= jnp.right_shift(r, 6)
    corr_ref[...] = jnp.where(allowed, jnp.maximum(c - r, 0).astype(F32) * (-2.0 * slope2), NEG_BIG)
    auxk_ref[...] = _alibi_aux(tk, slope2, 1)
    auxq_ref[...] = _alibi_aux(tq, slope2, -1)
    acc_ref[...] = jnp.zeros_like(acc_ref)
    m_ref[...] = jnp.full_like(m_ref, NEG_BIG)

    def q_rows(blk):
        return pl.ds(pl.multiple_of(blk * tq, tq), tq)

    def k_rows(blk, n):
        return pl.ds(pl.multiple_of(blk * tk, tk), n)

    def kv_len(kind):
        return tq if kind == KIND_LAST_HALF else tk

    def scores(t, kind, s_out, mx_out):
        blk_q, blk_k = qi_tab[t], j_tab[t]
        n = kv_len(kind)
        for mp in range(2):
            cols = slice(mp * HEAD_DIM, (mp + 1) * HEAD_DIM)
            kk = jnp.concatenate([k_ref[0, k_rows(blk_k, n), cols], auxk_ref[:n, :]], axis=1)
            qq = jnp.concatenate([q_ref[0, q_rows(blk_q), cols], auxq_ref[...]], axis=1)
            s = _dot_nt(kk, qq)
            if kind == KIND_FULL:
                s_out[mp] = s
                mx_out[mp] = jnp.max(s, axis=0, keepdims=True)
            else:
                sq = s[n - tq:, :] + corr_ref[...]
                s_out[mp, n - tq:n, :] = sq
                mx = jnp.max(sq, axis=0, keepdims=True)
                if n > tq:
                    s_out[mp, :n - tq, :] = s[:n - tq, :]
                    mx = jnp.maximum(mx, jnp.max(s[:n - tq, :], axis=0, keepdims=True))
                mx_out[mp] = mx

    def consume(t, kind, s_in, mx_in):
        blk_q, blk_k = qi_tab[t], j_tab[t]
        n = kv_len(kind)
        cj = slope2 * (blk_q * tq - blk_k * tk).astype(F32)
        vj = vt_ref[0, 0, :, k_rows(blk_k, n)]
        for mp in range(2):
            m_old = jnp.where(blk_k == 0, NEG_BIG, m_ref[mp])
            m_new = jnp.maximum(m_old, mx_in[mp] - cj)
            m_ref[mp] = m_new
            p_ref[mp, :n, :] = jnp.exp2(s_in[mp, :n, :] - (m_new + cj)).astype(BF16)
            acc_ref[mp] = jnp.exp2(m_old - m_new) * acc_ref[mp] + _dot(vj, p_ref[mp, :n, :])

    def finish(blk_q):
        lam = _lambda_full(lam_ref, lambda_init)
        o1 = acc_ref[0, :V_DIM, :] * (1.0 / acc_ref[0, V_DIM:V_DIM + 1, :])
        o2 = acc_ref[1, :V_DIM, :] * (1.0 / acc_ref[1, V_DIM:V_DIM + 1, :])
        o = o1 - lam * o2
        on = o * lax.rsqrt(jnp.mean(o * o, axis=0, keepdims=True) + SUBLN_EPS)
        o_ref[0, q_rows(blk_q), :] = (on.T * sw_ref[...] * (1.0 - lambda_init)).astype(BF16)

    def step(t, s_cur, mx_cur, s_nxt, mx_nxt):
        cur_half = kind_tab[t] == KIND_LAST_HALF
        for nxt_kind in (KIND_FULL, KIND_LAST_FULL, KIND_LAST_HALF):
            for cur_kind in (KIND_FULL, KIND_LAST_HALF):
                @pl.when((kind_tab[t + 1] == nxt_kind) & (cur_half == (cur_kind == KIND_LAST_HALF)))
                def _():
                    scores(t + 1, nxt_kind, s_nxt, mx_nxt)
                    consume(t, cur_kind, s_cur, mx_cur)

        @pl.when(kind_tab[t] != KIND_FULL)
        def _():
            finish(qi_tab[t])

    scores(0, KIND_LAST_HALF, s_a, mx_a)

    def pair(tt, carry):
        step(2 * tt, s_a, mx_a, s_b, mx_b)
        step(2 * tt + 1, s_b, mx_b, s_a, mx_a)
        return carry

    lax.fori_loop(0, nsteps // 2, pair, 0)
    if nsteps % 2:
        step(nsteps - 1, s_a, mx_a, s_b, mx_b)


def _attn_prompt(q, k, v, slopes, lam_vecs, subln_w, n_heads, lambda_init, tq):
    b, s, _ = q.shape
    tk = 2 * tq
    assert s % tk == 0
    nq = s // tq
    vt = jnp.swapaxes(v, 1, 2).reshape(b, n_heads, V_DIM, s)
    vt = jnp.concatenate([vt, jnp.ones((b, n_heads, ONES_ROWS, s), BF16)], axis=2)
    last_kind = (KIND_LAST_HALF, KIND_LAST_FULL)
    pairs = [(qi, j, last_kind[qi % 2] if j == qi // 2 else KIND_FULL)
             for qi in range(nq) for j in range(qi // 2 + 1)]
    pairs.append(pairs[-1])
    qi_tab, j_tab, kind_tab = (jnp.asarray(np.array([p[i] for p in pairs], np.int32))
                               for i in range(3))
    nsteps = len(pairs) - 1
    vrows = V_DIM + ONES_ROWS
    seq_spec = pl.BlockSpec((1, s, V_DIM), lambda bi, hi, *_: (bi, 0, hi))
    grid_spec = pltpu.PrefetchScalarGridSpec(
        num_scalar_prefetch=4,
        grid=(b, n_heads),
        in_specs=[seq_spec, seq_spec,
                  pl.BlockSpec((1, 1, vrows, s), lambda bi, hi, *_: (bi, hi, 0, 0)),
                  pl.BlockSpec((4, HEAD_DIM), lambda bi, hi, *_: (0, 0)),
                  pl.BlockSpec((1, V_DIM), lambda bi, hi, *_: (0, 0))],
        out_specs=seq_spec,
        scratch_shapes=[pltpu.VMEM((tq, tq), F32),
                        pltpu.VMEM((tk, HEAD_DIM), BF16),
                        pltpu.VMEM((tq, HEAD_DIM), BF16),
                        pltpu.VMEM((2, tk, tq), F32),
                        pltpu.VMEM((2, tk, tq), F32),
                        pltpu.VMEM((2, 1, tq), F32),
                        pltpu.VMEM((2, 1, tq), F32),
                        pltpu.VMEM((2, tk, tq), BF16),
                        pltpu.VMEM((2, 1, tq), F32),
                        pltpu.VMEM((2, vrows, tq), F32)],
    )
    return pl.pallas_call(
        functools.partial(_attn_prompt_kernel, tq=tq, tk=tk, nsteps=nsteps, lambda_init=lambda_init),
        grid_spec=grid_spec,
        out_shape=jax.ShapeDtypeStruct(q.shape, BF16),
        compiler_params=_cparams("arbitrary", "arbitrary"),
        name="attn_prompt",
    )(slopes, qi_tab, j_tab, kind_tab, q, k, vt, lam_vecs, subln_w.reshape(1, V_DIM))


def _attn_sample_kernel(slopes_ref, q_ref, kn_ref, vn_ref, kc_ref, vc_ref, lam_ref, sw_ref, o_ref,
                        *, past, n, lambda_init):
    h = pl.program_id(1)
    slope2 = slopes_ref[h] * LOG2E
    r_c = lax.broadcasted_iota(jnp.int32, (n, past), 0)
    c_c = lax.broadcasted_iota(jnp.int32, (n, past), 1)
    bias_c = (past + r_c - c_c).astype(F32) * (-slope2)
    r_n = lax.broadcasted_iota(jnp.int32, (n, n), 0)
    c_n = lax.broadcasted_iota(jnp.int32, (n, n), 1)
    bias_n = jnp.abs(r_n - c_n).astype(F32) * (-slope2)
    vc = vc_ref[0].astype(BF16)
    vn = vn_ref[0]
    outs = []
    for mp in range(2):
        sl = slice(mp * HEAD_DIM, (mp + 1) * HEAD_DIM)
        qm = q_ref[0, :, sl]
        s_c = _dot_nt(qm, kc_ref[0, :, sl].astype(BF16)) + bias_c
        s_n = _dot_nt(qm, kn_ref[0, :, sl]) + bias_n
        m = jnp.maximum(jnp.max(s_c, axis=-1, keepdims=True), jnp.max(s_n, axis=-1, keepdims=True))
        p_c = jnp.exp2(s_c - m)
        p_n = jnp.exp2(s_n - m)
        l = jnp.sum(p_c, axis=-1, keepdims=True) + jnp.sum(p_n, axis=-1, keepdims=True)
        acc = _dot(p_c.astype(BF16), vc) + _dot(p_n.astype(BF16), vn)
        outs.append(acc / l)
    lam = _lambda_full(lam_ref, lambda_init)
    o = outs[0] - lam * outs[1]
    o_ref[0] = (_rms(o, sw_ref[...], SUBLN_EPS) * (1.0 - lambda_init)).astype(BF16)


def _attn_sample(q, kn, vn, kc, vc, layer, slopes, lam_vecs, subln_w, n_heads, lambda_init):
    b, n, _ = q.shape
    past = kc.shape[1]
    new_spec = pl.BlockSpec((1, n, V_DIM), lambda bi, hi, sl: (bi, 0, hi))
    cache_spec = pl.BlockSpec((1, past, V_DIM), lambda bi, hi, sl: (layer * b + bi, 0, hi))
    grid_spec = pltpu.PrefetchScalarGridSpec(
        num_scalar_prefetch=1,
        grid=(b, n_heads),
        in_specs=[new_spec, new_spec, new_spec, cache_spec, cache_spec,
                  pl.BlockSpec((4, HEAD_DIM), lambda bi, hi, sl: (0, 0)),
                  pl.BlockSpec((1, V_DIM), lambda bi, hi, sl: (0, 0))],
        out_specs=new_spec,
    )
    return pl.pallas_call(
        functools.partial(_attn_sample_kernel, past=past, n=n, lambda_init=lambda_init),
        grid_spec=grid_spec,
        out_shape=jax.ShapeDtypeStruct(q.shape, BF16),
        compiler_params=_cparams("arbitrary", "arbitrary"),
        name="attn_sample",
    )(slopes, q, kn, vn, kc, vc, lam_vecs, subln_w.reshape(1, V_DIM))


def _gmlp_kernel(h_ref, wu_ref, wg_ref, lnw_ref, lnb_ref, ws_ref, bs_ref, *refs,
                 tm, width, rows_per_stream, want_vn):
    if want_vn:
        o_ref, vn_out_ref, g_ref = refs
    else:
        o_ref, g_ref = refs
    cb = 512
    gd = width // GM_GROUPS
    h = h_ref[...]
    rowsum = jnp.zeros((tm, 1), F32)
    for j in range(width // cb):
        g = _gelu(_dot(h, wg_ref[:, j * cb:(j + 1) * cb]))
        g_ref[:, j * cb:(j + 1) * cb] = g
        rowsum = rowsum + jnp.sum(g, axis=-1, keepdims=True)
    mean = rowsum / width
    sq = jnp.zeros((tm, 1), F32)
    for j in range(width // cb):
        xc = g_ref[:, j * cb:(j + 1) * cb] - mean
        sq = sq + jnp.sum(xc * xc, axis=-1, keepdims=True)
    rstd = lax.rsqrt(sq / width + NORM_EPS)

    r = lax.broadcasted_iota(jnp.int32, (GM_CHUNK, GM_CHUNK), 0)
    c = lax.broadcasted_iota(jnp.int32, (GM_CHUNK, GM_CHUNK), 1)
    causal = (c <= r) & ((c // rows_per_stream) == (r // rows_per_stream))
    for j in range(width // cb):
        cols = slice(j * cb, (j + 1) * cb)
        vn = (g_ref[:, cols] - mean) * rstd * lnw_ref[:, cols] + lnb_ref[:, cols]
        if want_vn:
            vn_out_ref[:, cols] = vn
        vnb = vn.astype(BF16)
        u = _gelu(_dot(h, wu_ref[:, cols]))
        for gg in range(cb // gd):
            grp = j * (cb // gd) + gg
            wsm = jnp.where(causal, ws_ref[grp], 0.0).astype(BF16)
            bias = bs_ref[grp]
            for ch in range(tm // GM_CHUNK):
                rows = slice(ch * GM_CHUNK, (ch + 1) * GM_CHUNK)
                mixed = _dot(wsm, vnb[rows, gg * gd:(gg + 1) * gd]) + bias
                o_ref[rows, j * cb + gg * gd:j * cb + (gg + 1) * gd] = (
                    u[rows, gg * gd:(gg + 1) * gd] * mixed).astype(BF16)


def _gmlp(h, w, col_u, col_g, lnw, lnb, ws_t, bs_t, tm, rows_per_stream, want_vn):
    t, k = h.shape
    width = lnw.shape[-1]
    out_shape = [jax.ShapeDtypeStruct((t, width), BF16)]
    out_specs = [pl.BlockSpec((tm, width), lambda i: (i, 0))]
    if want_vn:
        out_shape.append(jax.ShapeDtypeStruct((t, width), F32))
        out_specs.append(pl.BlockSpec((tm, width), lambda i: (i, 0)))
    one = pl.Buffered(1)
    res = pl.pallas_call(
        functools.partial(_gmlp_kernel, tm=tm, width=width, rows_per_stream=rows_per_stream,
                          want_vn=want_vn),
        grid=(t // tm,),
        in_specs=[pl.BlockSpec((tm, k), lambda i: (i, 0)),
                  pl.BlockSpec((k, width), lambda i: (0, col_u // width), pipeline_mode=one),
                  pl.BlockSpec((k, width), lambda i: (0, col_g // width), pipeline_mode=one),
                  pl.BlockSpec((1, width), lambda i: (0, 0)),
                  pl.BlockSpec((1, width), lambda i: (0, 0)),
                  pl.BlockSpec((GM_GROUPS, GM_CHUNK, GM_CHUNK), lambda i: (0, 0, 0)),
                  pl.BlockSpec((GM_GROUPS, GM_CHUNK, 1), lambda i: (0, 0, 0))],
        out_specs=out_specs,
        out_shape=out_shape,
        scratch_shapes=[pltpu.VMEM((tm, width), F32)],
        compiler_params=_cparams("arbitrary"),
        name="gmlp",
    )(h, w, w, lnw.reshape(1, width), lnb.reshape(1, width), ws_t, bs_t)
    return res


def _merge_kernel(h_ref, a_ref, g_ref, wa_ref, wb_ref, wba_ref, wbg_ref, o_ref):
    h = h_ref[...]
    ga = jax.nn.sigmoid(_dot(h, wa_ref[...]))
    gb = jax.nn.sigmoid(_dot(h, wb_ref[...]))
    o_ref[...] = (ga * _dot(a_ref[...], wba_ref[...]) + gb * _dot(g_ref[...], wbg_ref[...])).astype(BF16)


def _merge(h, attn, gm, w_in, col_a, col_b, wba, wbg, tm, tn):
    t, k = h.shape
    n = wba.shape[1]
    lhs = pl.BlockSpec((tm, k), lambda j, i: (i, 0))
    return pl.pallas_call(
        _merge_kernel,
        grid=(n // tn, t // tm),
        in_specs=[lhs, lhs, lhs,
                  pl.BlockSpec((k, tn), lambda j, i: (0, j + col_a // tn)),
                  pl.BlockSpec((k, tn), lambda j, i: (0, j + col_b // tn)),
                  pl.BlockSpec((k, tn), lambda j, i: (0, j)),
                  pl.BlockSpec((k, tn), lambda j, i: (0, j))],
        out_specs=pl.BlockSpec((tm, tn), lambda j, i: (i, j)),
        out_shape=jax.ShapeDtypeStruct((t, n), BF16),
        compiler_params=_cparams("arbitrary", "arbitrary"),
        name="merge",
    )(h, attn, gm, w_in, w_in, wba, wbg)


def _out_proj_kernel(m_ref, w_ref, x_ref, npost_ref, npre_ref, x1_ref, h2_ref):
    y = _dot(m_ref[...], w_ref[...])
    x1 = x_ref[...] + _rms(y, npost_ref[...], NORM_EPS)
    x1_ref[...] = x1
    h2_ref[...] = _rms(x1, npre_ref[...], NORM_EPS).astype(BF16)


def _out_proj(merged, w_out, x, npost, npre, tm):
    t, k = merged.shape
    d = w_out.shape[1]
    row = pl.BlockSpec((tm, d), lambda i: (i, 0))
    vec = pl.BlockSpec((1, d), lambda i: (0, 0))
    return pl.pallas_call(
        _out_proj_kernel,
        grid=(t // tm,),
        in_specs=[pl.BlockSpec((tm, k), lambda i: (i, 0)),
                  pl.BlockSpec((k, d), lambda i: (0, 0), pipeline_mode=pl.Buffered(1)),
                  row, vec, vec],
        out_specs=[row, row],
        out_shape=[jax.ShapeDtypeStruct((t, d), F32), jax.ShapeDtypeStruct((t, d), BF16)],
        compiler_params=_cparams("arbitrary"),
        name="out_proj",
    )(merged, w_out, x, npost.reshape(1, d), npre.reshape(1, d))


def _ffn_hidden_kernel(h_ref, wg_ref, wu_ref, o_ref):
    h = h_ref[...]
    o_ref[...] = (jax.nn.silu(_dot(h, wg_ref[...])) * _dot(h, wu_ref[...])).astype(BF16)


def _ffn_hidden(h2, wg, wu, tm, tn):
    t, k = h2.shape
    f = wg.shape[1]
    wspec = pl.BlockSpec((k, tn), lambda j, i: (0, j))
    return pl.pallas_call(
        _ffn_hidden_kernel,
        grid=(f // tn, t // tm),
        in_specs=[pl.BlockSpec((tm, k), lambda j, i: (i, 0)), wspec, wspec],
        out_specs=pl.BlockSpec((tm, tn), lambda j, i: (i, j)),
        out_shape=jax.ShapeDtypeStruct((t, f), BF16),
        compiler_params=_cparams("arbitrary", "arbitrary"),
        name="ffn_hidden",
    )(h2, wg, wu)


def _ffn_down_kernel(a_ref, w_ref, x1_ref, n_ref, o_ref):
    kk = pl.program_id(1)

    @pl.when(kk == 0)
    def _():
        o_ref[...] = _dot(a_ref[...], w_ref[...])

    @pl.when(kk > 0)
    def _():
        o_ref[...] += _dot(a_ref[...], w_ref[...])

    @pl.when(kk == pl.num_programs(1) - 1)
    def _():
        o_ref[...] = x1_ref[...] + _rms(o_ref[...], n_ref[...], NORM_EPS)


def _ffn_down(hidden, wd, x1, npost, tm, tk):
    t, f = hidden.shape
    d = wd.shape[1]
    row = pl.BlockSpec((tm, d), lambda i, kk: (i, 0))
    return pl.pallas_call(
        _ffn_down_kernel,
        grid=(t // tm, f // tk),
        in_specs=[pl.BlockSpec((tm, tk), lambda i, kk: (i, kk)),
                  pl.BlockSpec((tk, d), lambda i, kk: (kk, 0)),
                  row,
                  pl.BlockSpec((1, d), lambda i, kk: (0, 0))],
        out_specs=row,
        out_shape=jax.ShapeDtypeStruct((t, d), F32),
        compiler_params=_cparams("arbitrary", "arbitrary"),
        name="ffn_down",
    )(hidden, wd, x1, npost.reshape(1, d))


def _largest_tile(t, cap):
    tm = min(t, cap)
    while t % tm:
        tm //= 2
    return tm


def _layer(x, p, lambda_init, cache):
    b, s, d = x.shape
    t = b * s
    n_heads = d // V_DIM
    x2 = x.reshape(t, d)
    tm_big = _largest_tile(t, 1024)
    tm_mid = _largest_tile(t, 512)

    qk_w = n_heads * 2 * HEAD_DIM
    col_k, col_v = qk_w, 2 * qk_w
    col_u = col_v + n_heads * V_DIM
    col_g = col_u + d
    col_a = col_g + d
    col_b = col_a + d
    w_in = p["w_in"]
    tn = 1024
    h, k_f32, k_bf = _norm_proj_k(x2, p["norm_mix_pre"], w_in, col_k, qk_w, tm_mid)
    (q,) = _proj(h, w_in, 0, qk_w, tm_big, tn, scale=np.float32(HEAD_DIM ** -0.5 * LOG2E))
    v_f32, v_bf = _proj(h, w_in, col_v, n_heads * V_DIM, tm_big, tn, want_f32=True)

    slopes = 2.0 ** (-8.0 * jnp.arange(1, n_heads + 1, dtype=F32) / n_heads)
    lam_vecs = jnp.stack([p["lambda_q1"], p["lambda_k1"], p["lambda_q2"], p["lambda_k2"]])
    q3, k3, v3 = (a.reshape(b, s, -1) for a in (q, k_bf, v_bf))
    if cache is None:
        attn = _attn_prompt(q3, k3, v3, slopes, lam_vecs, p["subln_w"], n_heads, lambda_init,
                            _largest_tile(s, 512))
        rows_per_stream = GM_CHUNK
        ws_t = p["gm_ws"]
        bs_t = p["gm_bs"]
    else:
        ck, cv, layer = cache
        past = ck.shape[2]
        attn = _attn_sample(q3, k3, v3, ck.reshape(-1, past, d), cv.reshape(-1, past, d), layer,
                            slopes, lam_vecs, p["subln_w"], n_heads, lambda_init)
        rows_per_stream = s
        reps = GM_CHUNK // s
        ws_t = jnp.tile(p["gm_ws"][:, :s, :s], (1, reps, reps))
        bs_t = jnp.tile(p["gm_bs"][:, :s], (1, reps))
    attn = attn.reshape(t, -1)

    res = _gmlp(h, w_in, col_u, col_g, p["gm_ln_w"], p["gm_ln_b"], ws_t,
                bs_t.reshape(GM_GROUPS, GM_CHUNK, 1), _largest_tile(t, 256), rows_per_stream,
                want_vn=cache is not None)
    gm = res[0]
    vn = res[1] if cache is not None else None

    merged = _merge(h, attn, gm, w_in, col_a, col_b, p["w_branch_attn"], p["w_branch_gmlp"],
                    tm_big, 512)
    x1, h2 = _out_proj(merged, p["w_out"], x2, p["norm_mix_post"], p["norm_ffn_pre"], tm_mid)
    hidden = _ffn_hidden(h2, p["w_ffn_gate"], p["w_ffn_up"], tm_big, 512)
    y = _ffn_down(hidden, p["w_ffn_down"], x1, p["norm_ffn_post"], tm_big, 512)
    return y.reshape(b, s, d), k_f32, v_f32, vn


def kernel(x_prompt, x_sample, cache_k, cache_v, norm_mix_pre, norm_mix_post, w_in, lambda_q1, lambda_k1, lambda_q2, lambda_k2, subln_w, gm_ln_w, gm_ln_b, gm_ws, gm_bs, w_branch_attn, w_branch_gmlp, w_out, norm_ffn_pre, norm_ffn_post, w_ffn_gate, w_ffn_up, w_ffn_down):
    depth = w_in.shape[0]
    bp, sp, d = x_prompt.shape
    bs_, ss, _ = x_sample.shape
    n_heads = d // V_DIM
    y_p, y_s = x_prompt, x_sample
    kp, vp, ks, vs, gs = [], [], [], [], []
    for l in range(depth):
        p = dict(norm_mix_pre=norm_mix_pre[l], norm_mix_post=norm_mix_post[l],
                 w_in=w_in[l].astype(BF16),
                 lambda_q1=lambda_q1[l], lambda_k1=lambda_k1[l], lambda_q2=lambda_q2[l],
                 lambda_k2=lambda_k2[l], subln_w=subln_w[l], gm_ln_w=gm_ln_w[l], gm_ln_b=gm_ln_b[l],
                 gm_ws=gm_ws[l], gm_bs=gm_bs[l],
                 w_branch_attn=w_branch_attn[l].astype(BF16),
                 w_branch_gmlp=w_branch_gmlp[l].astype(BF16),
                 w_out=w_out[l].astype(BF16),
                 norm_ffn_pre=norm_ffn_pre[l], norm_ffn_post=norm_ffn_post[l],
                 w_ffn_gate=w_ffn_gate[l].astype(BF16), w_ffn_up=w_ffn_up[l].astype(BF16),
                 w_ffn_down=w_ffn_down[l].astype(BF16))
        lambda_init = 0.8 - 0.6 * math.exp(-0.3 * l)
        y_p, k1, v1, _ = _layer(y_p, p, lambda_init, None)
        y_s, k2, v2, g2 = _layer(y_s, p, lambda_init, (cache_k, cache_v, l))
        kp.append(k1.reshape(bp, sp, n_heads, 2, HEAD_DIM))
        vp.append(v1.reshape(bp, sp, n_heads, V_DIM))
        ks.append(k2.reshape(bs_, ss, n_heads, 2, HEAD_DIM))
        vs.append(v2.reshape(bs_, ss, n_heads, V_DIM))
        gs.append(g2.reshape(bs_, ss, GM_GROUPS, d // GM_GROUPS))
    return (y_p, y_s, jnp.stack(kp), jnp.stack(vp), jnp.stack(ks), jnp.stack(vs), jnp.stack(gs))
```

```python
import functools
import math

import jax
import jax.numpy as jnp
import numpy as np
from jax import lax
from jax.experimental import pallas as pl
from jax.experimental.pallas import tpu as pltpu

CHUNK = 64
HEAD_DIM = 128
V_DIM = 2 * HEAD_DIM
GM_CHUNK = 128
GM_GROUPS = 8
NORM_EPS = 1e-6
SUBLN_EPS = 1e-5
LOG2E = math.log2(math.e)
NEG_BIG = -1e30

V7X_VMEM_BYTES = 64 * 1024 * 1024
VMEM_LIMIT_BYTES = 56 * 1024 * 1024

BF16 = jnp.bfloat16
F32 = jnp.float32


def _cparams(*sem, flags=None):
    return pltpu.CompilerParams(dimension_semantics=sem, vmem_limit_bytes=VMEM_LIMIT_BYTES,
                                flags=flags)


def _rms(xf, w, eps):
    return xf * lax.rsqrt(jnp.mean(xf * xf, axis=-1, keepdims=True) + eps) * w


def _gelu(x):
    return 0.5 * x * (1.0 + lax.erf(x * np.float32(np.sqrt(0.5))))


def _dot(a, b):
    return jnp.dot(a, b, preferred_element_type=F32)


def _dot_nt(a, b):
    return lax.dot_general(a, b, (((1,), (1,)), ((), ())), preferred_element_type=F32)


def _proj_kernel(h_ref, w_ref, *o_refs, scale, want_f32, want_bf16):
    z = _dot(h_ref[...], w_ref[...])
    if scale is not None:
        z = z * scale
    i = 0
    if want_f32:
        o_refs[i][...] = z
        i += 1
    if want_bf16:
        o_refs[i][...] = z.astype(BF16)


def _proj(h, w, col0, ncols, tm, tn, *, scale=None, want_f32=False, want_bf16=True):
    t, k = h.shape
    off = col0 // tn
    out_shape, out_specs = [], []
    if want_f32:
        out_shape.append(jax.ShapeDtypeStruct((t, ncols), F32))
        out_specs.append(pl.BlockSpec((tm, tn), lambda n, m: (m, n)))
    if want_bf16:
        out_shape.append(jax.ShapeDtypeStruct((t, ncols), BF16))
        out_specs.append(pl.BlockSpec((tm, tn), lambda n, m: (m, n)))
    return pl.pallas_call(
        functools.partial(_proj_kernel, scale=scale, want_f32=want_f32, want_bf16=want_bf16),
        grid=(ncols // tn, t // tm),
        in_specs=[pl.BlockSpec((tm, k), lambda n, m: (m, 0)),
                  pl.BlockSpec((k, tn), lambda n, m: (0, n + off))],
        out_specs=out_specs,
        out_shape=out_shape,
        compiler_params=_cparams("arbitrary", "arbitrary"),
        name="in_proj",
    )(h, w)


def _norm_proj_k_kernel(x_ref, nw_ref, w_ref, h_ref, kf_ref, kb_ref, *, tm, slabs):
    h = _rms(x_ref[...], nw_ref[...], NORM_EPS).astype(BF16)
    h_ref[...] = h
    z = _dot(h, w_ref[...])
    kb_ref[...] = z.astype(BF16)
    for c in range(slabs):
        kf_ref[pl.ds(c, tm, stride=slabs), :] = z[:, c * HEAD_DIM:(c + 1) * HEAD_DIM]


def _norm_proj_k(x, nw, w, col0, ncols, tm):
    t, d = x.shape
    slabs = ncols // HEAD_DIM
    row_d = pl.BlockSpec((tm, d), lambda i: (i, 0))
    row_n = pl.BlockSpec((tm, ncols), lambda i: (i, 0))
    return pl.pallas_call(
        functools.partial(_norm_proj_k_kernel, tm=tm, slabs=slabs),
        grid=(t // tm,),
        in_specs=[row_d,
                  pl.BlockSpec((1, d), lambda i: (0, 0)),
                  pl.BlockSpec((d, ncols), lambda i: (0, col0 // ncols),
                               pipeline_mode=pl.Buffered(1))],
        out_specs=[row_d, pl.BlockSpec((tm * slabs, HEAD_DIM), lambda i: (i, 0)), row_n],
        out_shape=[jax.ShapeDtypeStruct((t, d), BF16),
                   jax.ShapeDtypeStruct((t * slabs, HEAD_DIM), F32),
                   jax.ShapeDtypeStruct((t, ncols), BF16)],
        compiler_params=_cparams("arbitrary"),
        name="norm_proj_k",
    )(x, nw.reshape(1, d), w)


ONES_ROWS = 16


def _proj_vt_kernel(h_ref, w_ref, vf_ref, vt_ref, *, n_heads):
    z = _dot(h_ref[...], w_ref[...])
    vf_ref[...] = z
    tm = z.shape[0]
    for hh in range(n_heads):
        vt_ref[0, hh, :V_DIM, :] = z[:, hh * V_DIM:(hh + 1) * V_DIM].T.astype(BF16)
        vt_ref[0, hh, V_DIM:, :] = jnp.ones((ONES_ROWS, tm), BF16)


def _proj_vt(h, w, col0, b, s, n_heads, tm):
    t, k = h.shape
    ncols = n_heads * V_DIM
    per_seq = s // tm
    return pl.pallas_call(
        functools.partial(_proj_vt_kernel, n_heads=n_heads),
        grid=(t // tm,),
        in_specs=[pl.BlockSpec((tm, k), lambda i: (i, 0)),
                  pl.BlockSpec((k, ncols), lambda i: (0, col0 // ncols),
                               pipeline_mode=pl.Buffered(1))],
        out_specs=[pl.BlockSpec((tm, ncols), lambda i: (i, 0)),
                   pl.BlockSpec((1, n_heads, V_DIM + ONES_ROWS, tm),
                                lambda i: (i // per_seq, 0, 0, i % per_seq))],
        out_shape=[jax.ShapeDtypeStruct((t, ncols), F32),
                   jax.ShapeDtypeStruct((b, n_heads, V_DIM + ONES_ROWS, s), BF16)],
        compiler_params=_cparams("arbitrary"),
        name="proj_vt",
    )(h, w)


def _lambda_full(lam_ref, lambda_init):
    lv = lam_ref[...]
    s1 = jnp.sum(lv[0:1] * lv[1:2], axis=-1, keepdims=True)
    s2 = jnp.sum(lv[2:3] * lv[3:4], axis=-1, keepdims=True)
    return jnp.exp(s1) - jnp.exp(s2) + lambda_init


AUX_SPLIT = 256


def _bf16_parts(x):
    a = x.astype(BF16).astype(F32)
    b = (x - a).astype(BF16).astype(F32)
    c = (x - a - b).astype(BF16).astype(F32)
    return a, b, c


def _alibi_aux(n, slope2, pos_side):
    col = lax.broadcasted_iota(jnp.int32, (n, HEAD_DIM), 1)
    pos = lax.broadcasted_iota(jnp.int32, (n, HEAD_DIM), 0)
    lo = jnp.bitwise_and(pos, AUX_SPLIT - 1)
    hi = pos - lo
    a, b, c = _bf16_parts(jnp.full((n, HEAD_DIM), slope2, F32))
    k3 = col - 3 * ((col >= 3).astype(jnp.int32) + (col >= 6).astype(jnp.int32)
                    + (col >= 9).astype(jnp.int32))
    slope_piece = jnp.where(k3 == 0, a, jnp.where(k3 == 1, b, c))
    first_half = (col < 3) | ((col >= 6) & (col < 9))
    pos_piece = jnp.where(first_half, hi, lo).astype(F32) * pos_side
    pos_cols = (col < 6) if pos_side > 0 else ((col >= 6) & (col < 12))
    slope_cols = ((col >= 6) & (col < 12)) if pos_side > 0 else (col < 6)
    return jnp.where(pos_cols, pos_piece, jnp.where(slope_cols, slope_piece, 0.0)).astype(BF16)


KIND_FULL = 0
KIND_LAST_FULL = 1
KIND_LAST_HALF = 2


def _attn_prompt_kernel(slopes_ref, qi_tab, j_tab, kind_tab, q_ref, k_ref, vt_ref, lam_ref, sw_ref,
                        o_ref, corr_ref, auxk_ref, auxq_ref, s_a, s_b, mx_a, mx_b, p_ref, m_ref,
                        acc_ref, *, tq, tk, nsteps, lambda_init):
    h = pl.program_id(1)
    slope2 = slopes_ref[h] * LOG2E

    c = lax.broadcasted_iota(jnp.int32, (tq, tq), 0)
    r = lax.broadcasted_iota(jnp.int32, (tq, tq), 1)
    allowed = jnp.right_shift(c, 6) <= jnp.right_shift(r, 6)
    corr_ref[...] = jnp.where(allowed, jnp.maximum(c - r, 0).astype(F32) * (-2.0 * slope2), NEG_BIG)
    auxk_ref[...] = _alibi_aux(tk, slope2, 1)
    auxq_ref[...] = _alibi_aux(tq, slope2, -1)
    acc_ref[...] = jnp.zeros_like(acc_ref)
    m_ref[...] = jnp.full_like(m_ref, NEG_BIG)

    def q_rows(blk):
        return pl.ds(pl.multiple_of(blk * tq, tq), tq)

    def k_rows(blk, n):
        return pl.ds(pl.multiple_of(blk * tk, tk), n)

    def kv_len(kind):
        return tq if kind == KIND_LAST_HALF else tk

    def scores(t, kind, s_out, mx_out):
        blk_q, blk_k = qi_tab[t], j_tab[t]
        n = kv_len(kind)
        for mp in range(2):
            cols = slice(mp * HEAD_DIM, (mp + 1) * HEAD_DIM)
            kk = jnp.concatenate([k_ref[0, k_rows(blk_k, n), cols], auxk_ref[:n, :]], axis=1)
            qq = jnp.concatenate([q_ref[0, q_rows(blk_q), cols], auxq_ref[...]], axis=1)
            s = _dot_nt(kk, qq)
            if kind == KIND_FULL:
                s_out[mp] = s
                mx_out[mp] = jnp.max(s, axis=0, keepdims=True)
            else:
                sq = s[n - tq:, :] + corr_ref[...]
                s_out[mp, n - tq:n, :] = sq
                mx = jnp.max(sq, axis=0, keepdims=True)
                if n > tq:
                    s_out[mp, :n - tq, :] = s[:n - tq, :]
                    mx = jnp.maximum(mx, jnp.max(s[:n - tq, :], axis=0, keepdims=True))
                mx_out[mp] = mx

    def consume(t, kind, s_in, mx_in):
        blk_q, blk_k = qi_tab[t], j_tab[t]
        n = kv_len(kind)
        cj = slope2 * (blk_q * tq - blk_k * tk).astype(F32)
        vj = vt_ref[0, 0, :, k_rows(blk_k, n)]
        for mp in range(2):
            m_old = jnp.where(blk_k == 0, NEG_BIG, m_ref[mp])
            m_new = jnp.maximum(m_old, mx_in[mp] - cj)
            m_ref[mp] = m_new
            p_ref[mp, :n, :] = jnp.exp2(s_in[mp, :n, :] - (m_new + cj)).astype(BF16)
            acc_ref[mp] = jnp.exp2(m_old - m_new) * acc_ref[mp] + _dot(vj, p_ref[mp, :n, :])

    def finish(blk_q):
        lam = _lambda_full(lam_ref, lambda_init)
        o1 = acc_ref[0, :V_DIM, :] * (1.0 / acc_ref[0, V_DIM:V_DIM + 1, :])
        o2 = acc_ref[1, :V_DIM, :] * (1.0 / acc_ref[1, V_DIM:V_DIM + 1, :])
        o = o1 - lam * o2
        on = o * lax.rsqrt(jnp.mean(o * o, axis=0, keepdims=True) + SUBLN_EPS)
        o_ref[0, q_rows(blk_q), :] = (on.T * sw_ref[...] * (1.0 - lambda_init)).astype(BF16)

    def step(t, s_cur, mx_cur, s_nxt, mx_nxt):
        cur_half = kind_tab[t] == KIND_LAST_HALF
        opens = (j_tab[t] == 0) & (t > 0)
        for nxt_kind in (KIND_FULL, KIND_LAST_FULL, KIND_LAST_HALF):
            for cur_kind, first in ((KIND_FULL, False), (KIND_FULL, True), (KIND_LAST_HALF, False)):
                @pl.when((kind_tab[t + 1] == nxt_kind) & (opens == first)
                         & (cur_half == (cur_kind == KIND_LAST_HALF)))
                def _():
                    scores(t + 1, nxt_kind, s_nxt, mx_nxt)
                    if first:
                        finish(qi_tab[t] - 1)
                    consume(t, cur_kind, s_cur, mx_cur)

    scores(0, KIND_LAST_HALF, s_a, mx_a)

    def pair(tt, carry):
        step(2 * tt, s_a, mx_a, s_b, mx_b)
        step(2 * tt + 1, s_b, mx_b, s_a, mx_a)
        return carry

    lax.fori_loop(0, nsteps // 2, pair, 0)
    if nsteps % 2:
        step(nsteps - 1, s_a, mx_a, s_b, mx_b)
    finish(qi_tab[nsteps - 1])


def _attn_prompt(q, k, vt, slopes, lam_vecs, subln_w, n_heads, lambda_init, tq):
    b, s, _ = q.shape
    tk = 2 * tq
    assert s % tk == 0
    nq = s // tq
    last_kind = (KIND_LAST_HALF, KIND_LAST_FULL)
    pairs = [(qi, j, last_kind[qi % 2] if j == qi // 2 else KIND_FULL)
             for qi in range(nq) for j in range(qi // 2 + 1)]
    pairs.append(pairs[-1])
    qi_tab, j_tab, kind_tab = (jnp.asarray(np.array([p[i] for p in pairs], np.int32))
                               for i in range(3))
    nsteps = len(pairs) - 1
    vrows = V_DIM + ONES_ROWS
    seq_spec = pl.BlockSpec((1, s, V_DIM), lambda bi, hi, *_: (bi, 0, hi))
    grid_spec = pltpu.PrefetchScalarGridSpec(
        num_scalar_prefetch=4,
        grid=(b, n_heads),
        in_specs=[seq_spec, seq_spec,
                  pl.BlockSpec((1, 1, vrows, s), lambda bi, hi, *_: (bi, hi, 0, 0)),
                  pl.BlockSpec((4, HEAD_DIM), lambda bi, hi, *_: (0, 0)),
                  pl.BlockSpec((1, V_DIM), lambda bi, hi, *_: (0, 0))],
        out_specs=seq_spec,
        scratch_shapes=[pltpu.VMEM((tq, tq), F32),
                        pltpu.VMEM((tk, HEAD_DIM), BF16),
                        pltpu.VMEM((tq, HEAD_DIM), BF16),
                        pltpu.VMEM((2, tk, tq), F32),
                        pltpu.VMEM((2, tk, tq), F32),
                        pltpu.VMEM((2, 1, tq), F32),
                        pltpu.VMEM((2, 1, tq), F32),
                        pltpu.VMEM((2, tk, tq), BF16),
                        pltpu.VMEM((2, 1, tq), F32),
                        pltpu.VMEM((2, vrows, tq), F32)],
    )
    return pl.pallas_call(
        functools.partial(_attn_prompt_kernel, tq=tq, tk=tk, nsteps=nsteps, lambda_init=lambda_init),
        grid_spec=grid_spec,
        out_shape=jax.ShapeDtypeStruct(q.shape, BF16),
        compiler_params=_cparams("arbitrary", "arbitrary"),
        name="attn_prompt",
    )(slopes, qi_tab, j_tab, kind_tab, q, k, vt, lam_vecs, subln_w.reshape(1, V_DIM))


def _attn_sample_kernel(slopes_ref, q_ref, kn_ref, vn_ref, kc_ref, vc_ref, lam_ref, sw_ref, o_ref,
                        *, past, n, lambda_init):
    h = pl.program_id(1)
    slope2 = slopes_ref[h] * LOG2E
    r_c = lax.broadcasted_iota(jnp.int32, (n, past), 0)
    c_c = lax.broadcasted_iota(jnp.int32, (n, past), 1)
    bias_c = (past + r_c - c_c).astype(F32) * (-slope2)
    r_n = lax.broadcasted_iota(jnp.int32, (n, n), 0)
    c_n = lax.broadcasted_iota(jnp.int32, (n, n), 1)
    bias_n = jnp.abs(r_n - c_n).astype(F32) * (-slope2)
    vc = vc_ref[0].astype(BF16)
    vn = vn_ref[0]
    outs = []
    for mp in range(2):
        sl = slice(mp * HEAD_DIM, (mp + 1) * HEAD_DIM)
        qm = q_ref[0, :, sl]
        s_c = _dot_nt(qm, kc_ref[0, :, sl].astype(BF16)) + bias_c
        s_n = _dot_nt(qm, kn_ref[0, :, sl]) + bias_n
        m = jnp.maximum(jnp.max(s_c, axis=-1, keepdims=True), jnp.max(s_n, axis=-1, keepdims=True))
        p_c = jnp.exp2(s_c - m)
        p_n = jnp.exp2(s_n - m)
        l = jnp.sum(p_c, axis=-1, keepdims=True) + jnp.sum(p_n, axis=-1, keepdims=True)
        acc = _dot(p_c.astype(BF16), vc) + _dot(p_n.astype(BF16), vn)
        outs.append(acc / l)
    lam = _lambda_full(lam_ref, lambda_init)
    o = outs[0] - lam * outs[1]
    o_ref[0] = (_rms(o, sw_ref[...], SUBLN_EPS) * (1.0 - lambda_init)).astype(BF16)


def _attn_sample(q, kn, vn, kc, vc, layer, slopes, lam_vecs, subln_w, n_heads, lambda_init):
    b, n, _ = q.shape
    past = kc.shape[1]
    new_spec = pl.BlockSpec((1, n, V_DIM), lambda bi, hi, sl: (bi, 0, hi))
    cache_spec = pl.BlockSpec((1, past, V_DIM), lambda bi, hi, sl: (layer * b + bi, 0, hi))
    grid_spec = pltpu.PrefetchScalarGridSpec(
        num_scalar_prefetch=1,
        grid=(b, n_heads),
        in_specs=[new_spec, new_spec, new_spec, cache_spec, cache_spec,
                  pl.BlockSpec((4, HEAD_DIM), lambda bi, hi, sl: (0, 0)),
                  pl.BlockSpec((1, V_DIM), lambda bi, hi, sl: (0, 0))],
        out_specs=new_spec,
    )
    return pl.pallas_call(
        functools.partial(_attn_sample_kernel, past=past, n=n, lambda_init=lambda_init),
        grid_spec=grid_spec,
        out_shape=jax.ShapeDtypeStruct(q.shape, BF16),
        compiler_params=_cparams("arbitrary", "arbitrary"),
        name="attn_sample",
    )(slopes, q, kn, vn, kc, vc, lam_vecs, subln_w.reshape(1, V_DIM))


def _gmlp_kernel(h_ref, wu_ref, wg_ref, lnw_ref, lnb_ref, ws_ref, bs_ref, *refs,
                 tm, width, rows_per_stream, want_vn):
    if want_vn:
        o_ref, vn_out_ref, g_ref = refs
    else:
        o_ref, g_ref = refs
    cb = 512
    gd = width // GM_GROUPS
    h = h_ref[...]
    rowsum = jnp.zeros((tm, 1), F32)
    for j in range(width // cb):
        g = _gelu(_dot(h, wg_ref[:, j * cb:(j + 1) * cb]))
        g_ref[:, j * cb:(j + 1) * cb] = g
        rowsum = rowsum + jnp.sum(g, axis=-1, keepdims=True)
    mean = rowsum / width
    sq = jnp.zeros((tm, 1), F32)
    for j in range(width // cb):
        xc = g_ref[:, j * cb:(j + 1) * cb] - mean
        sq = sq + jnp.sum(xc * xc, axis=-1, keepdims=True)
    rstd = lax.rsqrt(sq / width + NORM_EPS)

    r = lax.broadcasted_iota(jnp.int32, (GM_CHUNK, GM_CHUNK), 0)
    c = lax.broadcasted_iota(jnp.int32, (GM_CHUNK, GM_CHUNK), 1)
    causal = (c <= r) & ((c // rows_per_stream) == (r // rows_per_stream))
    for j in range(width // cb):
        cols = slice(j * cb, (j + 1) * cb)
        vn = (g_ref[:, cols] - mean) * rstd * lnw_ref[:, cols] + lnb_ref[:, cols]
        if want_vn:
            vn_out_ref[:, cols] = vn
        vnb = vn.astype(BF16)
        u = _gelu(_dot(h, wu_ref[:, cols]))
        for gg in range(cb // gd):
            grp = j * (cb // gd) + gg
            wsm = jnp.where(causal, ws_ref[grp], 0.0).astype(BF16)
            bias = bs_ref[grp]
            for ch in range(tm // GM_CHUNK):
                rows = slice(ch * GM_CHUNK, (ch + 1) * GM_CHUNK)
                mixed = _dot(wsm, vnb[rows, gg * gd:(gg + 1) * gd]) + bias
                o_ref[rows, j * cb + gg * gd:j * cb + (gg + 1) * gd] = (
                    u[rows, gg * gd:(gg + 1) * gd] * mixed).astype(BF16)


def _gmlp(h, w, col_u, col_g, lnw, lnb, ws_t, bs_t, tm, rows_per_stream, want_vn):
    t, k = h.shape
    width = lnw.shape[-1]
    out_shape = [jax.ShapeDtypeStruct((t, width), BF16)]
    out_specs = [pl.BlockSpec((tm, width), lambda i: (i, 0))]
    if want_vn:
        out_shape.append(jax.ShapeDtypeStruct((t, width), F32))
        out_specs.append(pl.BlockSpec((tm, width), lambda i: (i, 0)))
    one = pl.Buffered(1)
    res = pl.pallas_call(
        functools.partial(_gmlp_kernel, tm=tm, width=width, rows_per_stream=rows_per_stream,
                          want_vn=want_vn),
        grid=(t // tm,),
        in_specs=[pl.BlockSpec((tm, k), lambda i: (i, 0)),
                  pl.BlockSpec((k, width), lambda i: (0, col_u // width), pipeline_mode=one),
                  pl.BlockSpec((k, width), lambda i: (0, col_g // width), pipeline_mode=one),
                  pl.BlockSpec((1, width), lambda i: (0, 0)),
                  pl.BlockSpec((1, width), lambda i: (0, 0)),
                  pl.BlockSpec((GM_GROUPS, GM_CHUNK, GM_CHUNK), lambda i: (0, 0, 0)),
                  pl.BlockSpec((GM_GROUPS, GM_CHUNK, 1), lambda i: (0, 0, 0))],
        out_specs=out_specs,
        out_shape=out_shape,
        scratch_shapes=[pltpu.VMEM((tm, width), F32)],
        compiler_params=_cparams("arbitrary"),
        name="gmlp",
    )(h, w, w, lnw.reshape(1, width), lnb.reshape(1, width), ws_t, bs_t)
    return res


def _merge_kernel(h_ref, a_ref, g_ref, wa_ref, wb_ref, wba_ref, wbg_ref, o_ref):
    h = h_ref[...]
    ga = jax.nn.sigmoid(_dot(h, wa_ref[...]))
    gb = jax.nn.sigmoid(_dot(h, wb_ref[...]))
    o_ref[...] = (ga * _dot(a_ref[...], wba_ref[...]) + gb * _dot(g_ref[...], wbg_ref[...])).astype(BF16)


def _merge(h, attn, gm, w_in, col_a, col_b, wba, wbg, tm, tn):
    t, k = h.shape
    n = wba.shape[1]
    lhs = pl.BlockSpec((tm, k), lambda j, i: (i, 0))
    return pl.pallas_call(
        _merge_kernel,
        grid=(n // tn, t // tm),
        in_specs=[lhs, lhs, lhs,
                  pl.BlockSpec((k, tn), lambda j, i: (0, j + col_a // tn)),
                  pl.BlockSpec((k, tn), lambda j, i: (0, j + col_b // tn)),
                  pl.BlockSpec((k, tn), lambda j, i: (0, j)),
                  pl.BlockSpec((k, tn), lambda j, i: (0, j))],
        out_specs=pl.BlockSpec((tm, tn), lambda j, i: (i, j)),
        out_shape=jax.ShapeDtypeStruct((t, n), BF16),
        compiler_params=_cparams("arbitrary", "arbitrary"),
        name="merge",
    )(h, attn, gm, w_in, w_in, wba, wbg)


def _out_proj_kernel(m_ref, w_ref, x_ref, npost_ref, npre_ref, x1_ref, h2_ref):
    y = _dot(m_ref[...], w_ref[...])
    x1 = x_ref[...] + _rms(y, npost_ref[...], NORM_EPS)
    x1_ref[...] = x1
    h2_ref[...] = _rms(x1, npre_ref[...], NORM_EPS).astype(BF16)


def _out_proj(merged, w_out, x, npost, npre, tm):
    t, k = merged.shape
    d = w_out.shape[1]
    row = pl.BlockSpec((tm, d), lambda i: (i, 0))
    vec = pl.BlockSpec((1, d), lambda i: (0, 0))
    return pl.pallas_call(
        _out_proj_kernel,
        grid=(t // tm,),
        in_specs=[pl.BlockSpec((tm, k), lambda i: (i, 0)),
                  pl.BlockSpec((k, d), lambda i: (0, 0), pipeline_mode=pl.Buffered(1)),
                  row, vec, vec],
        out_specs=[row, row],
        out_shape=[jax.ShapeDtypeStruct((t, d), F32), jax.ShapeDtypeStruct((t, d), BF16)],
        compiler_params=_cparams("arbitrary"),
        name="out_proj",
    )(merged, w_out, x, npost.reshape(1, d), npre.reshape(1, d))


def _ffn_hidden_kernel(h_ref, wg_ref, wu_ref, o_ref):
    h = h_ref[...]
    o_ref[...] = (jax.nn.silu(_dot(h, wg_ref[...])) * _dot(h, wu_ref[...])).astype(BF16)


def _ffn_hidden(h2, wg, wu, tm, tn):
    t, k = h2.shape
    f = wg.shape[1]
    wspec = pl.BlockSpec((k, tn), lambda j, i: (0, j))
    return pl.pallas_call(
        _ffn_hidden_kernel,
        grid=(f // tn, t // tm),
        in_specs=[pl.BlockSpec((tm, k), lambda j, i: (i, 0)), wspec, wspec],
        out_specs=pl.BlockSpec((tm, tn), lambda j, i: (i, j)),
        out_shape=jax.ShapeDtypeStruct((t, f), BF16),
        compiler_params=_cparams("arbitrary", "arbitrary"),
        name="ffn_hidden",
    )(h2, wg, wu)


def _ffn_down_kernel(a_ref, w_ref, x1_ref, n_ref, o_ref):
    kk = pl.program_id(1)
    last = pl.num_programs(1) - 1

    @pl.when(kk == 0)
    def _():
        o_ref[...] = _dot(a_ref[...], w_ref[...])

    @pl.when((kk > 0) & (kk < last))
    def _():
        o_ref[...] += _dot(a_ref[...], w_ref[...])

    @pl.when(kk == last)
    def _():
        f = o_ref[...] + _dot(a_ref[...], w_ref[...])
        o_ref[...] = x1_ref[...] + _rms(f, n_ref[...], NORM_EPS)


def _ffn_down(hidden, wd, x1, npost, tm, tk):
    t, f = hidden.shape
    d = wd.shape[1]
    assert f // tk >= 2
    row = pl.BlockSpec((tm, d), lambda i, kk: (i, 0))
    return pl.pallas_call(
        _ffn_down_kernel,
        grid=(t // tm, f // tk),
        in_specs=[pl.BlockSpec((tm, tk), lambda i, kk: (i, kk)),
                  pl.BlockSpec((tk, d), lambda i, kk: (kk, 0)),
                  row,
                  pl.BlockSpec((1, d), lambda i, kk: (0, 0))],
        out_specs=row,
        out_shape=jax.ShapeDtypeStruct((t, d), F32),
        compiler_params=_cparams("arbitrary", "arbitrary"),
        name="ffn_down",
    )(hidden, wd, x1, npost.reshape(1, d))


def _largest_tile(t, cap):
    tm = min(t, cap)
    while t % tm:
        tm //= 2
    return tm


def _layer(x, p, lambda_init, cache):
    b, s, d = x.shape
    t = b * s
    n_heads = d // V_DIM
    x2 = x.reshape(t, d)
    tm_big = _largest_tile(t, 1024)
    tm_mid = _largest_tile(t, 512)

    qk_w = n_heads * 2 * HEAD_DIM
    col_k, col_v = qk_w, 2 * qk_w
    col_u = col_v + n_heads * V_DIM
    col_g = col_u + d
    col_a = col_g + d
    col_b = col_a + d
    w_in = p["w_in"]
    tn = 1024
    h, k_f32, k_bf = _norm_proj_k(x2, p["norm_mix_pre"], w_in, col_k, qk_w, tm_mid)
    (q,) = _proj(h, w_in, 0, qk_w, tm_big, tn, scale=np.float32(HEAD_DIM ** -0.5 * LOG2E))

    slopes = 2.0 ** (-8.0 * jnp.arange(1, n_heads + 1, dtype=F32) / n_heads)
    lam_vecs = jnp.stack([p["lambda_q1"], p["lambda_k1"], p["lambda_q2"], p["lambda_k2"]])
    q3, k3 = q.reshape(b, s, -1), k_bf.reshape(b, s, -1)
    if cache is None:
        v_f32, vt = _proj_vt(h, w_in, col_v, b, s, n_heads, tm_mid)
        attn = _attn_prompt(q3, k3, vt, slopes, lam_vecs, p["subln_w"], n_heads, lambda_init,
                            _largest_tile(s, 512))
        rows_per_stream = GM_CHUNK
        ws_t = p["gm_ws"]
        bs_t = p["gm_bs"]
    else:
        ck, cv, layer = cache
        past = ck.shape[2]
        v_f32, v_bf = _proj(h, w_in, col_v, n_heads * V_DIM, tm_big, tn, want_f32=True)
        v3 = v_bf.reshape(b, s, -1)
        attn = _attn_sample(q3, k3, v3, ck.reshape(-1, past, d), cv.reshape(-1, past, d), layer,
                            slopes, lam_vecs, p["subln_w"], n_heads, lambda_init)
        rows_per_stream = s
        reps = GM_CHUNK // s
        ws_t = jnp.tile(p["gm_ws"][:, :s, :s], (1, reps, reps))
        bs_t = jnp.tile(p["gm_bs"][:, :s], (1, reps))
    attn = attn.reshape(t, -1)

    res = _gmlp(h, w_in, col_u, col_g, p["gm_ln_w"], p["gm_ln_b"], ws_t,
                bs_t.reshape(GM_GROUPS, GM_CHUNK, 1), _largest_tile(t, 256), rows_per_stream,
                want_vn=cache is not None)
    gm = res[0]
    vn = res[1] if cache is not None else None

    merged = _merge(h, attn, gm, w_in, col_a, col_b, p["w_branch_attn"], p["w_branch_gmlp"],
                    tm_big, 512)
    x1, h2 = _out_proj(merged, p["w_out"], x2, p["norm_mix_post"], p["norm_ffn_pre"], tm_mid)
    hidden = _ffn_hidden(h2, p["w_ffn_gate"], p["w_ffn_up"], tm_big, 512)
    y = _ffn_down(hidden, p["w_ffn_down"], x1, p["norm_ffn_post"], tm_big, 512)
    return y.reshape(b, s, d), k_f32, v_f32, vn


def kernel(x_prompt, x_sample, cache_k, cache_v, norm_mix_pre, norm_mix_post, w_in, lambda_q1, lambda_k1, lambda_q2, lambda_k2, subln_w, gm_ln_w, gm_ln_b, gm_ws, gm_bs, w_branch_attn, w_branch_gmlp, w_out, norm_ffn_pre, norm_ffn_post, w_ffn_gate, w_ffn_up, w_ffn_down):
    depth = w_in.shape[0]
    bp, sp, d = x_prompt.shape
    bs_, ss, _ = x_sample.shape
    n_heads = d // V_DIM
    y_p, y_s = x_prompt, x_sample
    kp, vp, ks, vs, gs = [], [], [], [], []
    for l in range(depth):
        p = dict(norm_mix_pre=norm_mix_pre[l], norm_mix_post=norm_mix_post[l],
                 w_in=w_in[l].astype(BF16),
                 lambda_q1=lambda_q1[l], lambda_k1=lambda_k1[l], lambda_q2=lambda_q2[l],
                 lambda_k2=lambda_k2[l], subln_w=subln_w[l], gm_ln_w=gm_ln_w[l], gm_ln_b=gm_ln_b[l],
                 gm_ws=gm_ws[l], gm_bs=gm_bs[l],
                 w_branch_attn=w_branch_attn[l].astype(BF16),
                 w_branch_gmlp=w_branch_gmlp[l].astype(BF16),
                 w_out=w_out[l].astype(BF16),
                 norm_ffn_pre=norm_ffn_pre[l], norm_ffn_post=norm_ffn_post[l],
                 w_ffn_gate=w_ffn_gate[l].astype(BF16), w_ffn_up=w_ffn_up[l].astype(BF16),
                 w_ffn_down=w_ffn_down[l].astype(BF16))
        lambda_init = 0.8 - 0.6 * math.exp(-0.3 * l)
        y_p, k1, v1, _ = _layer(y_p, p, lambda_init, None)
        y_s, k2, v2, g2 = _layer(y_s, p, lambda_init, (cache_k, cache_v, l))
        kp.append(k1.reshape(bp, sp, n_heads, 2, HEAD_DIM))
        vp.append(v1.reshape(bp, sp, n_heads, V_DIM))
        ks.append(k2.reshape(bs_, ss, n_heads, 2, HEAD_DIM))
        vs.append(v2.reshape(bs_, ss, n_heads, V_DIM))
        gs.append(g2.reshape(bs_, ss, GM_GROUPS, d // GM_GROUPS))
    return (y_p, y_s, jnp.stack(kp), jnp.stack(vp), jnp.stack(ks), jnp.stack(vs), jnp.stack(gs))
```

```python
import functools
import math

import jax
import jax.numpy as jnp
import numpy as np
from jax import lax
from jax.experimental import pallas as pl
from jax.experimental.pallas import tpu as pltpu

CHUNK = 64
HEAD_DIM = 128
V_DIM = 2 * HEAD_DIM
GM_CHUNK = 128
GM_GROUPS = 8
NORM_EPS = 1e-6
SUBLN_EPS = 1e-5
LOG2E = math.log2(math.e)
NEG_BIG = -1e30

V7X_VMEM_BYTES = 64 * 1024 * 1024
VMEM_LIMIT_BYTES = 56 * 1024 * 1024

BF16 = jnp.bfloat16
F32 = jnp.float32


def _cparams(*sem, flags=None):
    return pltpu.CompilerParams(dimension_semantics=sem, vmem_limit_bytes=VMEM_LIMIT_BYTES,
                                flags=flags)


def _rms(xf, w, eps):
    return xf * lax.rsqrt(jnp.mean(xf * xf, axis=-1, keepdims=True) + eps) * w


def _gelu(x):
    return 0.5 * x * (1.0 + lax.erf(x * np.float32(np.sqrt(0.5))))


def _dot(a, b):
    return jnp.dot(a, b, preferred_element_type=F32)


def _dot_nt(a, b):
    return lax.dot_general(a, b, (((1,), (1,)), ((), ())), preferred_element_type=F32)


def _proj_kernel(h_ref, w_ref, *o_refs, scale, want_f32, want_bf16):
    z = _dot(h_ref[...], w_ref[...])
    if scale is not None:
        z = z * scale
    i = 0
    if want_f32:
        o_refs[i][...] = z
        i += 1
    if want_bf16:
        o_refs[i][...] = z.astype(BF16)


def _proj(h, w, col0, ncols, tm, tn, *, scale=None, want_f32=False, want_bf16=True):
    t, k = h.shape
    off = col0 // tn
    out_shape, out_specs = [], []
    if want_f32:
        out_shape.append(jax.ShapeDtypeStruct((t, ncols), F32))
        out_specs.append(pl.BlockSpec((tm, tn), lambda n, m: (m, n)))
    if want_bf16:
        out_shape.append(jax.ShapeDtypeStruct((t, ncols), BF16))
        out_specs.append(pl.BlockSpec((tm, tn), lambda n, m: (m, n)))
    return pl.pallas_call(
        functools.partial(_proj_kernel, scale=scale, want_f32=want_f32, want_bf16=want_bf16),
        grid=(ncols // tn, t // tm),
        in_specs=[pl.BlockSpec((tm, k), lambda n, m: (m, 0)),
                  pl.BlockSpec((k, tn), lambda n, m: (0, n + off))],
        out_specs=out_specs,
        out_shape=out_shape,
        compiler_params=_cparams("arbitrary", "arbitrary"),
        name="in_proj",
    )(h, w)


def _norm_proj_k_kernel(x_ref, nw_ref, w_ref, h_ref, kf_ref, kb_ref, *, tm, slabs):
    h = _rms(x_ref[...], nw_ref[...], NORM_EPS).astype(BF16)
    h_ref[...] = h
    z = _dot(h, w_ref[...])
    kb_ref[...] = z.astype(BF16)
    for c in range(slabs):
        kf_ref[pl.ds(c, tm, stride=slabs), :] = z[:, c * HEAD_DIM:(c + 1) * HEAD_DIM]


def _norm_proj_k(x, nw, w, col0, ncols, tm):
    t, d = x.shape
    slabs = ncols // HEAD_DIM
    row_d = pl.BlockSpec((tm, d), lambda i: (i, 0))
    row_n = pl.BlockSpec((tm, ncols), lambda i: (i, 0))
    return pl.pallas_call(
        functools.partial(_norm_proj_k_kernel, tm=tm, slabs=slabs),
        grid=(t // tm,),
        in_specs=[row_d,
                  pl.BlockSpec((1, d), lambda i: (0, 0)),
                  pl.BlockSpec((d, ncols), lambda i: (0, col0 // ncols),
                               pipeline_mode=pl.Buffered(1))],
        out_specs=[row_d, pl.BlockSpec((tm * slabs, HEAD_DIM), lambda i: (i, 0)), row_n],
        out_shape=[jax.ShapeDtypeStruct((t, d), BF16),
                   jax.ShapeDtypeStruct((t * slabs, HEAD_DIM), F32),
                   jax.ShapeDtypeStruct((t, ncols), BF16)],
        compiler_params=_cparams("arbitrary"),
        name="norm_proj_k",
    )(x, nw.reshape(1, d), w)


ONES_ROWS = 16


def _proj_vt_kernel(h_ref, w_ref, vf_ref, vt_ref, *, n_heads):
    z = _dot(h_ref[...], w_ref[...])
    vf_ref[...] = z
    tm = z.shape[0]
    for hh in range(n_heads):
        vt_ref[0, hh, :V_DIM, :] = z[:, hh * V_DIM:(hh + 1) * V_DIM].T.astype(BF16)
        vt_ref[0, hh, V_DIM:, :] = jnp.ones((ONES_ROWS, tm), BF16)


def _proj_vt(h, w, col0, b, s, n_heads, tm):
    t, k = h.shape
    ncols = n_heads * V_DIM
    per_seq = s // tm
    return pl.pallas_call(
        functools.partial(_proj_vt_kernel, n_heads=n_heads),
        grid=(t // tm,),
        in_specs=[pl.BlockSpec((tm, k), lambda i: (i, 0)),
                  pl.BlockSpec((k, ncols), lambda i: (0, col0 // ncols),
                               pipeline_mode=pl.Buffered(1))],
        out_specs=[pl.BlockSpec((tm, ncols), lambda i: (i, 0)),
                   pl.BlockSpec((1, n_heads, V_DIM + ONES_ROWS, tm),
                                lambda i: (i // per_seq, 0, 0, i % per_seq))],
        out_shape=[jax.ShapeDtypeStruct((t, ncols), F32),
                   jax.ShapeDtypeStruct((b, n_heads, V_DIM + ONES_ROWS, s), BF16)],
        compiler_params=_cparams("arbitrary"),
        name="proj_vt",
    )(h, w)


def _lambda_full(lam_ref, lambda_init):
    lv = lam_ref[...]
    s1 = jnp.sum(lv[0:1] * lv[1:2], axis=-1, keepdims=True)
    s2 = jnp.sum(lv[2:3] * lv[3:4], axis=-1, keepdims=True)
    return jnp.exp(s1) - jnp.exp(s2) + lambda_init


AUX_SPLIT = 256


def _bf16_parts(x):
    a = x.astype(BF16).astype(F32)
    b = (x - a).astype(BF16).astype(F32)
    c = (x - a - b).astype(BF16).astype(F32)
    return a, b, c


def _alibi_aux(n, slope2, pos_side):
    col = lax.broadcasted_iota(jnp.int32, (n, HEAD_DIM), 1)
    pos = lax.broadcasted_iota(jnp.int32, (n, HEAD_DIM), 0)
    lo = jnp.bitwise_and(pos, AUX_SPLIT - 1)
    hi = pos - lo
    a, b, c = _bf16_parts(jnp.full((n, HEAD_DIM), slope2, F32))
    k3 = col - 3 * ((col >= 3).astype(jnp.int32) + (col >= 6).astype(jnp.int32)
                    + (col >= 9).astype(jnp.int32))
    slope_piece = jnp.where(k3 == 0, a, jnp.where(k3 == 1, b, c))
    first_half = (col < 3) | ((col >= 6) & (col < 9))
    pos_piece = jnp.where(first_half, hi, lo).astype(F32) * pos_side
    pos_cols = (col < 6) if pos_side > 0 else ((col >= 6) & (col < 12))
    slope_cols = ((col >= 6) & (col < 12)) if pos_side > 0 else (col < 6)
    return jnp.where(pos_cols, pos_piece, jnp.where(slope_cols, slope_piece, 0.0)).astype(BF16)


KIND_FULL = 0
KIND_LAST_FULL = 1
KIND_LAST_HALF = 2


def _attn_prompt_kernel(slopes_ref, qi_tab, j_tab, kind_tab, q_ref, k_ref, vt_ref, lam_ref, sw_ref,
                        o_ref, corr_ref, auxk_ref, auxq_ref, s_a, s_b, mx_a, mx_b, p_ref, m_ref,
                        acc_ref, *, tq, tk, nsteps, lambda_init):
    h = pl.program_id(1)
    slope2 = slopes_ref[h] * LOG2E

    c = lax.broadcasted_iota(jnp.int32, (tq, tq), 0)
    r = lax.broadcasted_iota(jnp.int32, (tq, tq), 1)
    allowed = jnp.right_shift(c, 6) <= jnp.right_shift(r, 6)
    corr_ref[...] = jnp.where(allowed, jnp.maximum(c - r, 0).astype(F32) * (-2.0 * slope2), NEG_BIG)
    auxk_ref[...] = _alibi_aux(tk, slope2, 1)
    auxq_ref[...] = _alibi_aux(tq, slope2, -1)
    acc_ref[...] = jnp.zeros_like(acc_ref)
    m_ref[...] = jnp.full_like(m_ref, NEG_BIG)

    def q_rows(blk):
        return pl.ds(pl.multiple_of(blk * tq, tq), tq)

    def k_rows(blk, n):
        return pl.ds(pl.multiple_of(blk * tk, tk), n)

    def kv_len(kind):
        return tq if kind == KIND_LAST_HALF else tk

    def scores(t, kind, s_out, mx_out):
        blk_q, blk_k = qi_tab[t], j_tab[t]
        n = kv_len(kind)
        for mp in range(2):
            cols = slice(mp * HEAD_DIM, (mp + 1) * HEAD_DIM)
            kk = jnp.concatenate([k_ref[0, k_rows(blk_k, n), cols], auxk_ref[:n, :]], axis=1)
            qq = jnp.concatenate([q_ref[0, q_rows(blk_q), cols], auxq_ref[...]], axis=1)
            s = _dot_nt(kk, qq)
            if kind == KIND_FULL:
                s_out[mp] = s
                mx_out[mp] = jnp.max(s, axis=0, keepdims=True)
            else:
                sq = s[n - tq:, :] + corr_ref[...]
                s_out[mp, n - tq:n, :] = sq
                mx = jnp.max(sq, axis=0, keepdims=True)
                if n > tq:
                    s_out[mp, :n - tq, :] = s[:n - tq, :]
                    mx = jnp.maximum(mx, jnp.max(s[:n - tq, :], axis=0, keepdims=True))
                mx_out[mp] = mx

    def consume(t, kind, s_in, mx_in):
        blk_q, blk_k = qi_tab[t], j_tab[t]
        n = kv_len(kind)
        cj = slope2 * (blk_q * tq - blk_k * tk).astype(F32)
        vj = vt_ref[0, 0, :, k_rows(blk_k, n)]
        for mp in range(2):
            m_old = jnp.where(blk_k == 0, NEG_BIG, m_ref[mp])
            m_new = jnp.maximum(m_old, mx_in[mp] - cj)
            m_ref[mp] = m_new
            p_ref[mp, :n, :] = jnp.exp2(s_in[mp, :n, :] - (m_new + cj)).astype(BF16)
            acc_ref[mp] = jnp.exp2(m_old - m_new) * acc_ref[mp] + _dot(vj, p_ref[mp, :n, :])

    def finish(blk_q):
        lam = _lambda_full(lam_ref, lambda_init)
        o1 = acc_ref[0, :V_DIM, :] * (1.0 / acc_ref[0, V_DIM:V_DIM + 1, :])
        o2 = acc_ref[1, :V_DIM, :] * (1.0 / acc_ref[1, V_DIM:V_DIM + 1, :])
        o = o1 - lam * o2
        on = o * lax.rsqrt(jnp.mean(o * o, axis=0, keepdims=True) + SUBLN_EPS)
        o_ref[0, q_rows(blk_q), :] = (on.T * sw_ref[...] * (1.0 - lambda_init)).astype(BF16)

    def step(t, s_cur, mx_cur, s_nxt, mx_nxt):
        cur_half = kind_tab[t] == KIND_LAST_HALF
        opens = (j_tab[t] == 0) & (t > 0)
        for nxt_kind in (KIND_FULL, KIND_LAST_FULL, KIND_LAST_HALF):
            for cur_kind, first in ((KIND_FULL, False), (KIND_FULL, True), (KIND_LAST_HALF, False)):
                @pl.when((kind_tab[t + 1] == nxt_kind) & (opens == first)
                         & (cur_half == (cur_kind == KIND_LAST_HALF)))
                def _():
                    scores(t + 1, nxt_kind, s_nxt, mx_nxt)
                    if first:
                        finish(qi_tab[t] - 1)
                    consume(t, cur_kind, s_cur, mx_cur)

    scores(0, KIND_LAST_HALF, s_a, mx_a)

    def pair(tt, carry):
        step(2 * tt, s_a, mx_a, s_b, mx_b)
        step(2 * tt + 1, s_b, mx_b, s_a, mx_a)
        return carry

    lax.fori_loop(0, nsteps // 2, pair, 0)
    if nsteps % 2:
        step(nsteps - 1, s_a, mx_a, s_b, mx_b)
    finish(qi_tab[nsteps - 1])


def _attn_prompt(q, k, vt, slopes, lam_vecs, subln_w, n_heads, lambda_init, tq):
    b, s, _ = q.shape
    tk = 2 * tq
    assert s % tk == 0
    nq = s // tq
    last_kind = (KIND_LAST_HALF, KIND_LAST_FULL)
    pairs = [(qi, j, last_kind[qi % 2] if j == qi // 2 else KIND_FULL)
             for qi in range(nq) for j in range(qi // 2 + 1)]
    pairs.append(pairs[-1])
    qi_tab, j_tab, kind_tab = (jnp.asarray(np.array([p[i] for p in pairs], np.int32))
                               for i in range(3))
    nsteps = len(pairs) - 1
    vrows = V_DIM + ONES_ROWS
    seq_spec = pl.BlockSpec((1, s, V_DIM), lambda bi, hi, *_: (bi, 0, hi))
    grid_spec = pltpu.PrefetchScalarGridSpec(
        num_scalar_prefetch=4,
        grid=(b, n_heads),
        in_specs=[seq_spec, seq_spec,
                  pl.BlockSpec((1, 1, vrows, s), lambda bi, hi, *_: (bi, hi, 0, 0)),
                  pl.BlockSpec((4, HEAD_DIM), lambda bi, hi, *_: (0, 0)),
                  pl.BlockSpec((1, V_DIM), lambda bi, hi, *_: (0, 0))],
        out_specs=seq_spec,
        scratch_shapes=[pltpu.VMEM((tq, tq), F32),
                        pltpu.VMEM((tk, HEAD_DIM), BF16),
                        pltpu.VMEM((tq, HEAD_DIM), BF16),
                        pltpu.VMEM((2, tk, tq), F32),
                        pltpu.VMEM((2, tk, tq), F32),
                        pltpu.VMEM((2, 1, tq), F32),
                        pltpu.VMEM((2, 1, tq), F32),
                        pltpu.VMEM((2, tk, tq), BF16),
                        pltpu.VMEM((2, 1, tq), F32),
                        pltpu.VMEM((2, vrows, tq), F32)],
    )
    return pl.pallas_call(
        functools.partial(_attn_prompt_kernel, tq=tq, tk=tk, nsteps=nsteps, lambda_init=lambda_init),
        grid_spec=grid_spec,
        out_shape=jax.ShapeDtypeStruct(q.shape, BF16),
        compiler_params=_cparams("arbitrary", "arbitrary"),
        name="attn_prompt",
    )(slopes, qi_tab, j_tab, kind_tab, q, k, vt, lam_vecs, subln_w.reshape(1, V_DIM))


def _attn_sample_kernel(slopes_ref, q_ref, kn_ref, vn_ref, kc_ref, vc_ref, lam_ref, sw_ref, o_ref,
                        *, past, n, lambda_init):
    h = pl.program_id(1)
    slope2 = slopes_ref[h] * LOG2E
    r_c = lax.broadcasted_iota(jnp.int32, (n, past), 0)
    c_c = lax.broadcasted_iota(jnp.int32, (n, past), 1)
    bias_c = (past + r_c - c_c).astype(F32) * (-slope2)
    r_n = lax.broadcasted_iota(jnp.int32, (n, n), 0)
    c_n = lax.broadcasted_iota(jnp.int32, (n, n), 1)
    bias_n = jnp.abs(r_n - c_n).astype(F32) * (-slope2)
    vc = vc_ref[0].astype(BF16)
    vn = vn_ref[0]
    outs = []
    for mp in range(2):
        sl = slice(mp * HEAD_DIM, (mp + 1) * HEAD_DIM)
        qm = q_ref[0, :, sl]
        s_c = _dot_nt(qm, kc_ref[0, :, sl].astype(BF16)) + bias_c
        s_n = _dot_nt(qm, kn_ref[0, :, sl]) + bias_n
        m = jnp.maximum(jnp.max(s_c, axis=-1, keepdims=True), jnp.max(s_n, axis=-1, keepdims=True))
        p_c = jnp.exp2(s_c - m)
        p_n = jnp.exp2(s_n - m)
        l = jnp.sum(p_c, axis=-1, keepdims=True) + jnp.sum(p_n, axis=-1, keepdims=True)
        acc = _dot(p_c.astype(BF16), vc) + _dot(p_n.astype(BF16), vn)
        outs.append(acc / l)
    lam = _lambda_full(lam_ref, lambda_init)
    o = outs[0] - lam * outs[1]
    o_ref[0] = (_rms(o, sw_ref[...], SUBLN_EPS) * (1.0 - lambda_init)).astype(BF16)


def _attn_sample(q, kn, vn, kc, vc, layer, slopes, lam_vecs, subln_w, n_heads, lambda_init):
    b, n, _ = q.shape
    past = kc.shape[1]
    new_spec = pl.BlockSpec((1, n, V_DIM), lambda bi, hi, sl: (bi, 0, hi))
    cache_spec = pl.BlockSpec((1, past, V_DIM), lambda bi, hi, sl: (layer * b + bi, 0, hi))
    grid_spec = pltpu.PrefetchScalarGridSpec(
        num_scalar_prefetch=1,
        grid=(b, n_heads),
        in_specs=[new_spec, new_spec, new_spec, cache_spec, cache_spec,
                  pl.BlockSpec((4, HEAD_DIM), lambda bi, hi, sl: (0, 0)),
                  pl.BlockSpec((1, V_DIM), lambda bi, hi, sl: (0, 0))],
        out_specs=new_spec,
    )
    return pl.pallas_call(
        functools.partial(_attn_sample_kernel, past=past, n=n, lambda_init=lambda_init),
        grid_spec=grid_spec,
        out_shape=jax.ShapeDtypeStruct(q.shape, BF16),
        compiler_params=_cparams("arbitrary", "arbitrary"),
        name="attn_sample",
    )(slopes, q, kn, vn, kc, vc, lam_vecs, subln_w.reshape(1, V_DIM))


def _gmlp_kernel(h_ref, wu_ref, wg_ref, lnw_ref, lnb_ref, ws_ref, bs_ref, *refs,
                 tm, width, rows_per_stream, want_vn):
    if want_vn:
        o_ref, vn_out_ref, g_ref = refs
    else:
        o_ref, g_ref = refs
    cb = 512
    gd = width // GM_GROUPS
    h = h_ref[...]
    rowsum = jnp.zeros((tm, 1), F32)
    for j in range(width // cb):
        g = _gelu(_dot(h, wg_ref[:, j * cb:(j + 1) * cb]))
        g_ref[:, j * cb:(j + 1) * cb] = g
        rowsum = rowsum + jnp.sum(g, axis=-1, keepdims=True)
    mean = rowsum / width
    sq = jnp.zeros((tm, 1), F32)
    for j in range(width // cb):
        xc = g_ref[:, j * cb:(j + 1) * cb] - mean
        sq = sq + jnp.sum(xc * xc, axis=-1, keepdims=True)
    rstd = lax.rsqrt(sq / width + NORM_EPS)

    r = lax.broadcasted_iota(jnp.int32, (GM_CHUNK, GM_CHUNK), 0)
    c = lax.broadcasted_iota(jnp.int32, (GM_CHUNK, GM_CHUNK), 1)
    causal = (c <= r) & ((c // rows_per_stream) == (r // rows_per_stream))
    for j in range(width // cb):
        cols = slice(j * cb, (j + 1) * cb)
        vn = (g_ref[:, cols] - mean) * rstd * lnw_ref[:, cols] + lnb_ref[:, cols]
        if want_vn:
            vn_out_ref[:, cols] = vn
        vnb = vn.astype(BF16)
        u = _gelu(_dot(h, wu_ref[:, cols]))
        for gg in range(cb // gd):
            grp = j * (cb // gd) + gg
            wsm = jnp.where(causal, ws_ref[grp], 0.0).astype(BF16)
            bias = bs_ref[grp]
            for ch in range(tm // GM_CHUNK):
                rows = slice(ch * GM_CHUNK, (ch + 1) * GM_CHUNK)
                mixed = _dot(wsm, vnb[rows, gg * gd:(gg + 1) * gd]) + bias
                o_ref[rows, j * cb + gg * gd:j * cb + (gg + 1) * gd] = (
                    u[rows, gg * gd:(gg + 1) * gd] * mixed).astype(BF16)


def _gmlp(h, w, col_u, col_g, lnw, lnb, ws_t, bs_t, tm, rows_per_stream, want_vn):
    t, k = h.shape
    width = lnw.shape[-1]
    out_shape = [jax.ShapeDtypeStruct((t, width), BF16)]
    out_specs = [pl.BlockSpec((tm, width), lambda i: (i, 0))]
    if want_vn:
        out_shape.append(jax.ShapeDtypeStruct((t, width), F32))
        out_specs.append(pl.BlockSpec((tm, width), lambda i: (i, 0)))
    one = pl.Buffered(1)
    res = pl.pallas_call(
        functools.partial(_gmlp_kernel, tm=tm, width=width, rows_per_stream=rows_per_stream,
                          want_vn=want_vn),
        grid=(t // tm,),
        in_specs=[pl.BlockSpec((tm, k), lambda i: (i, 0)),
                  pl.BlockSpec((k, width), lambda i: (0, col_u // width), pipeline_mode=one),
                  pl.BlockSpec((k, width), lambda i: (0, col_g // width), pipeline_mode=one),
                  pl.BlockSpec((1, width), lambda i: (0, 0)),
                  pl.BlockSpec((1, width), lambda i: (0, 0)),
                  pl.BlockSpec((GM_GROUPS, GM_CHUNK, GM_CHUNK), lambda i: (0, 0, 0)),
                  pl.BlockSpec((GM_GROUPS, GM_CHUNK, 1), lambda i: (0, 0, 0))],
        out_specs=out_specs,
        out_shape=out_shape,
        scratch_shapes=[pltpu.VMEM((tm, width), F32)],
        compiler_params=_cparams("arbitrary"),
        name="gmlp",
    )(h, w, w, lnw.reshape(1, width), lnb.reshape(1, width), ws_t, bs_t)
    return res


def _merge_kernel(h_ref, a_ref, g_ref, wa_ref, wb_ref, wba_ref, wbg_ref, o_ref):
    h = h_ref[...]
    ga = jax.nn.sigmoid(_dot(h, wa_ref[...]))
    gb = jax.nn.sigmoid(_dot(h, wb_ref[...]))
    o_ref[...] = (ga * _dot(a_ref[...], wba_ref[...]) + gb * _dot(g_ref[...], wbg_ref[...])).astype(BF16)


def _merge(h, attn, gm, w_in, col_a, col_b, wba, wbg, tm, tn):
    t, k = h.shape
    n = wba.shape[1]
    lhs = pl.BlockSpec((tm, k), lambda j, i: (i, 0))
    return pl.pallas_call(
        _merge_kernel,
        grid=(n // tn, t // tm),
        in_specs=[lhs, lhs, lhs,
                  pl.BlockSpec((k, tn), lambda j, i: (0, j + col_a // tn)),
                  pl.BlockSpec((k, tn), lambda j, i: (0, j + col_b // tn)),
                  pl.BlockSpec((k, tn), lambda j, i: (0, j)),
                  pl.BlockSpec((k, tn), lambda j, i: (0, j))],
        out_specs=pl.BlockSpec((tm, tn), lambda j, i: (i, j)),
        out_shape=jax.ShapeDtypeStruct((t, n), BF16),
        compiler_params=_cparams("arbitrary", "arbitrary"),
        name="merge",
    )(h, attn, gm, w_in, w_in, wba, wbg)


def _out_proj_kernel(m_ref, w_ref, x_ref, npost_ref, npre_ref, x1_ref, h2_ref):
    y = _dot(m_ref[...], w_ref[...])
    x1 = x_ref[...] + _rms(y, npost_ref[...], NORM_EPS)
    x1_ref[...] = x1
    h2_ref[...] = _rms(x1, npre_ref[...], NORM_EPS).astype(BF16)


def _out_proj(merged, w_out, x, npost, npre, tm):
    t, k = merged.shape
    d = w_out.shape[1]
    row = pl.BlockSpec((tm, d), lambda i: (i, 0))
    vec = pl.BlockSpec((1, d), lambda i: (0, 0))
    return pl.pallas_call(
        _out_proj_kernel,
        grid=(t // tm,),
        in_specs=[pl.BlockSpec((tm, k), lambda i: (i, 0)),
                  pl.BlockSpec((k, d), lambda i: (0, 0), pipeline_mode=pl.Buffered(1)),
                  row, vec, vec],
        out_specs=[row, row],
        out_shape=[jax.ShapeDtypeStruct((t, d), F32), jax.ShapeDtypeStruct((t, d), BF16)],
        compiler_params=_cparams("arbitrary"),
        name="out_proj",
    )(merged, w_out, x, npost.reshape(1, d), npre.reshape(1, d))


def _ffn_hidden_kernel(h_ref, wg_ref, wu_ref, o_ref):
    h = h_ref[...]
    o_ref[...] = (jax.nn.silu(_dot(h, wg_ref[...])) * _dot(h, wu_ref[...])).astype(BF16)


def _ffn_hidden(h2, wg, wu, tm, tn):
    t, k = h2.shape
    f = wg.shape[1]
    wspec = pl.BlockSpec((k, tn), lambda j, i: (0, j))
    return pl.pallas_call(
        _ffn_hidden_kernel,
        grid=(f // tn, t // tm),
        in_specs=[pl.BlockSpec((tm, k), lambda j, i: (i, 0)), wspec, wspec],
        out_specs=pl.BlockSpec((tm, tn), lambda j, i: (i, j)),
        out_shape=jax.ShapeDtypeStruct((t, f), BF16),
        compiler_params=_cparams("arbitrary", "arbitrary"),
        name="ffn_hidden",
    )(h2, wg, wu)


def _ffn_down_kernel(a_ref, w_ref, x1_ref, n_ref, o_ref, x1_all, *, x1_chunks):
    kk = pl.program_id(1)
    last = pl.num_programs(1) - 1
    cw = x1_ref.shape[1]
    for c in range(x1_chunks):
        @pl.when(kk == c)
        def _():
            x1_all[:, c * cw:(c + 1) * cw] = x1_ref[...]

    @pl.when(kk == 0)
    def _():
        o_ref[...] = _dot(a_ref[...], w_ref[...])

    @pl.when((kk > 0) & (kk < last))
    def _():
        o_ref[...] += _dot(a_ref[...], w_ref[...])

    @pl.when(kk == last)
    def _():
        f = o_ref[...] + _dot(a_ref[...], w_ref[...])
        o_ref[...] = x1_all[...] + _rms(f, n_ref[...], NORM_EPS)


def _ffn_down(hidden, wd, x1, npost, tm, tk):
    t, f = hidden.shape
    d = wd.shape[1]
    nk = f // tk
    assert nk >= 2
    x1_chunks = 1
    while x1_chunks * 2 <= nk and d % (x1_chunks * 2 * 128) == 0:
        x1_chunks *= 2
    row = pl.BlockSpec((tm, d), lambda i, kk: (i, 0))
    return pl.pallas_call(
        functools.partial(_ffn_down_kernel, x1_chunks=x1_chunks),
        grid=(t // tm, nk),
        in_specs=[pl.BlockSpec((tm, tk), lambda i, kk: (i, kk)),
                  pl.BlockSpec((tk, d), lambda i, kk: (kk, 0)),
                  pl.BlockSpec((tm, d // x1_chunks),
                               lambda i, kk: (i, jnp.minimum(kk, x1_chunks - 1))),
                  pl.BlockSpec((1, d), lambda i, kk: (0, 0))],
        out_specs=row,
        out_shape=jax.ShapeDtypeStruct((t, d), F32),
        scratch_shapes=[pltpu.VMEM((tm, d), F32)],
        compiler_params=_cparams("arbitrary", "arbitrary"),
        name="ffn_down",
    )(hidden, wd, x1, npost.reshape(1, d))


def _largest_tile(t, cap):
    tm = min(t, cap)
    while t % tm:
        tm //= 2
    return tm


def _layer(x, p, lambda_init, cache):
    b, s, d = x.shape
    t = b * s
    n_heads = d // V_DIM
    x2 = x.reshape(t, d)
    tm_big = _largest_tile(t, 1024)
    tm_mid = _largest_tile(t, 512)

    qk_w = n_heads * 2 * HEAD_DIM
    col_k, col_v = qk_w, 2 * qk_w
    col_u = col_v + n_heads * V_DIM
    col_g = col_u + d
    col_a = col_g + d
    col_b = col_a + d
    w_in = p["w_in"]
    tn = 1024
    h, k_f32, k_bf = _norm_proj_k(x2, p["norm_mix_pre"], w_in, col_k, qk_w, tm_mid)
    (q,) = _proj(h, w_in, 0, qk_w, tm_mid, qk_w, scale=np.float32(HEAD_DIM ** -0.5 * LOG2E))

    slopes = 2.0 ** (-8.0 * jnp.arange(1, n_heads + 1, dtype=F32) / n_heads)
    lam_vecs = jnp.stack([p["lambda_q1"], p["lambda_k1"], p["lambda_q2"], p["lambda_k2"]])
    q3, k3 = q.reshape(b, s, -1), k_bf.reshape(b, s, -1)
    if cache is None:
        v_f32, vt = _proj_vt(h, w_in, col_v, b, s, n_heads, tm_mid)
        attn = _attn_prompt(q3, k3, vt, slopes, lam_vecs, p["subln_w"], n_heads, lambda_init,
                            _largest_tile(s, 512))
        rows_per_stream = GM_CHUNK
        ws_t = p["gm_ws"]
        bs_t = p["gm_bs"]
    else:
        ck, cv, layer = cache
        past = ck.shape[2]
        v_f32, v_bf = _proj(h, w_in, col_v, n_heads * V_DIM, tm_big, tn, want_f32=True)
        v3 = v_bf.reshape(b, s, -1)
        attn = _attn_sample(q3, k3, v3, ck.reshape(-1, past, d), cv.reshape(-1, past, d), layer,
                            slopes, lam_vecs, p["subln_w"], n_heads, lambda_init)
        rows_per_stream = s
        reps = GM_CHUNK // s
        ws_t = jnp.tile(p["gm_ws"][:, :s, :s], (1, reps, reps))
        bs_t = jnp.tile(p["gm_bs"][:, :s], (1, reps))
    attn = attn.reshape(t, -1)

    res = _gmlp(h, w_in, col_u, col_g, p["gm_ln_w"], p["gm_ln_b"], ws_t,
                bs_t.reshape(GM_GROUPS, GM_CHUNK, 1), _largest_tile(t, 256), rows_per_stream,
                want_vn=cache is not None)
    gm = res[0]
    vn = res[1] if cache is not None else None

    merged = _merge(h, attn, gm, w_in, col_a, col_b, p["w_branch_attn"], p["w_branch_gmlp"],
                    tm_big, 512)
    x1, h2 = _out_proj(merged, p["w_out"], x2, p["norm_mix_post"], p["norm_ffn_pre"], tm_mid)
    hidden = _ffn_hidden(h2, p["w_ffn_gate"], p["w_ffn_up"], tm_big, 512)
    y = _ffn_down(hidden, p["w_ffn_down"], x1, p["norm_ffn_post"], tm_big, 512)
    return y.reshape(b, s, d), k_f32, v_f32, vn


def kernel(x_prompt, x_sample, cache_k, cache_v, norm_mix_pre, norm_mix_post, w_in, lambda_q1, lambda_k1, lambda_q2, lambda_k2, subln_w, gm_ln_w, gm_ln_b, gm_ws, gm_bs, w_branch_attn, w_branch_gmlp, w_out, norm_ffn_pre, norm_ffn_post, w_ffn_gate, w_ffn_up, w_ffn_down):
    depth = w_in.shape[0]
    bp, sp, d = x_prompt.shape
    bs_, ss, _ = x_sample.shape
    n_heads = d // V_DIM
    y_p, y_s = x_prompt, x_sample
    kp, vp, ks, vs, gs = [], [], [], [], []
    for l in range(depth):
        p = dict(norm_mix_pre=norm_mix_pre[l], norm_mix_post=norm_mix_post[l],
                 w_in=w_in[l].astype(BF16),
                 lambda_q1=lambda_q1[l], lambda_k1=lambda_k1[l], lambda_q2=lambda_q2[l],
                 lambda_k2=lambda_k2[l], subln_w=subln_w[l], gm_ln_w=gm_ln_w[l], gm_ln_b=gm_ln_b[l],
                 gm_ws=gm_ws[l], gm_bs=gm_bs[l],
                 w_branch_attn=w_branch_attn[l].astype(BF16),
                 w_branch_gmlp=w_branch_gmlp[l].astype(BF16),
                 w_out=w_out[l].astype(BF16),
                 norm_ffn_pre=norm_ffn_pre[l], norm_ffn_post=norm_ffn_post[l],
                 w_ffn_gate=w_ffn_gate[l].astype(BF16), w_ffn_up=w_ffn_up[l].astype(BF16),
                 w_ffn_down=w_ffn_down[l].astype(BF16))
        lambda_init = 0.8 - 0.6 * math.exp(-0.3 * l)
        y_p, k1, v1, _ = _layer(y_p, p, lambda_init, None)
        y_s, k2, v2, g2 = _layer(y_s, p, lambda_init, (cache_k, cache_v, l))
        kp.append(k1.reshape(bp, sp, n_heads, 2, HEAD_DIM))
        vp.append(v1.reshape(bp, sp, n_heads, V_DIM))
        ks.append(k2.reshape(bs_, ss, n_heads, 2, HEAD_DIM))
        vs.append(v2.reshape(bs_, ss, n_heads, V_DIM))
        gs.append(g2.reshape(bs_, ss, GM_GROUPS, d // GM_GROUPS))
    return (y_p, y_s, jnp.stack(kp), jnp.stack(vp), jnp.stack(ks), jnp.stack(vs), jnp.stack(gs))
```

```python
import functools
import math

import jax
import jax.numpy as jnp
import numpy as np
from jax import lax
from jax.experimental import pallas as pl
from jax.experimental.pallas import tpu as pltpu

CHUNK = 64
HEAD_DIM = 128
V_DIM = 2 * HEAD_DIM
GM_CHUNK = 128
GM_GROUPS = 8
NORM_EPS = 1e-6
SUBLN_EPS = 1e-5
LOG2E = math.log2(math.e)
NEG_BIG = -1e30

V7X_VMEM_BYTES = 64 * 1024 * 1024
VMEM_LIMIT_BYTES = 56 * 1024 * 1024

BF16 = jnp.bfloat16
F32 = jnp.float32


def _cparams(*sem, flags=None):
    return pltpu.CompilerParams(dimension_semantics=sem, vmem_limit_bytes=VMEM_LIMIT_BYTES,
                                flags=flags)


def _rms(xf, w, eps):
    return xf * lax.rsqrt(jnp.mean(xf * xf, axis=-1, keepdims=True) + eps) * w


def _gelu(x):
    return 0.5 * x * (1.0 + lax.erf(x * np.float32(np.sqrt(0.5))))


def _dot(a, b):
    return jnp.dot(a, b, preferred_element_type=F32)


def _dot_nt(a, b):
    return lax.dot_general(a, b, (((1,), (1,)), ((), ())), preferred_element_type=F32)


def _proj_kernel(h_ref, w_ref, *o_refs, scale, want_f32, want_bf16):
    z = _dot(h_ref[...], w_ref[...])
    if scale is not None:
        z = z * scale
    i = 0
    if want_f32:
        o_refs[i][...] = z
        i += 1
    if want_bf16:
        o_refs[i][...] = z.astype(BF16)


def _proj(h, w, col0, ncols, tm, tn, *, scale=None, want_f32=False, want_bf16=True):
    t, k = h.shape
    off = col0 // tn
    out_shape, out_specs = [], []
    if want_f32:
        out_shape.append(jax.ShapeDtypeStruct((t, ncols), F32))
        out_specs.append(pl.BlockSpec((tm, tn), lambda n, m: (m, n)))
    if want_bf16:
        out_shape.append(jax.ShapeDtypeStruct((t, ncols), BF16))
        out_specs.append(pl.BlockSpec((tm, tn), lambda n, m: (m, n)))
    return pl.pallas_call(
        functools.partial(_proj_kernel, scale=scale, want_f32=want_f32, want_bf16=want_bf16),
        grid=(ncols // tn, t // tm),
        in_specs=[pl.BlockSpec((tm, k), lambda n, m: (m, 0)),
                  pl.BlockSpec((k, tn), lambda n, m: (0, n + off))],
        out_specs=out_specs,
        out_shape=out_shape,
        compiler_params=_cparams("arbitrary", "arbitrary"),
        name="in_proj",
    )(h, w)


def _norm_proj_k_kernel(x_ref, nw_ref, w_ref, h_ref, kf_ref, kb_ref, *, tm, slabs):
    h = _rms(x_ref[...], nw_ref[...], NORM_EPS).astype(BF16)
    h_ref[...] = h
    z = _dot(h, w_ref[...])
    kb_ref[...] = z.astype(BF16)
    for c in range(slabs):
        kf_ref[pl.ds(c, tm, stride=slabs), :] = z[:, c * HEAD_DIM:(c + 1) * HEAD_DIM]


def _norm_proj_k(x, nw, w, col0, ncols, tm):
    t, d = x.shape
    slabs = ncols // HEAD_DIM
    row_d = pl.BlockSpec((tm, d), lambda i: (i, 0))
    row_n = pl.BlockSpec((tm, ncols), lambda i: (i, 0))
    return pl.pallas_call(
        functools.partial(_norm_proj_k_kernel, tm=tm, slabs=slabs),
        grid=(t // tm,),
        in_specs=[row_d,
                  pl.BlockSpec((1, d), lambda i: (0, 0)),
                  pl.BlockSpec((d, ncols), lambda i: (0, col0 // ncols),
                               pipeline_mode=pl.Buffered(1))],
        out_specs=[row_d, pl.BlockSpec((tm * slabs, HEAD_DIM), lambda i: (i, 0)), row_n],
        out_shape=[jax.ShapeDtypeStruct((t, d), BF16),
                   jax.ShapeDtypeStruct((t * slabs, HEAD_DIM), F32),
                   jax.ShapeDtypeStruct((t, ncols), BF16)],
        compiler_params=_cparams("arbitrary"),
        name="norm_proj_k",
    )(x, nw.reshape(1, d), w)


ONES_ROWS = 16


def _proj_vt_kernel(h_ref, w_ref, vf_ref, vt_ref, *, n_heads):
    z = _dot(h_ref[...], w_ref[...])
    vf_ref[...] = z
    tm = z.shape[0]
    for hh in range(n_heads):
        vt_ref[0, hh, :V_DIM, :] = z[:, hh * V_DIM:(hh + 1) * V_DIM].T.astype(BF16)
        vt_ref[0, hh, V_DIM:, :] = jnp.ones((ONES_ROWS, tm), BF16)


def _proj_vt(h, w, col0, b, s, n_heads, tm):
    t, k = h.shape
    ncols = n_heads * V_DIM
    per_seq = s // tm
    return pl.pallas_call(
        functools.partial(_proj_vt_kernel, n_heads=n_heads),
        grid=(t // tm,),
        in_specs=[pl.BlockSpec((tm, k), lambda i: (i, 0)),
                  pl.BlockSpec((k, ncols), lambda i: (0, col0 // ncols),
                               pipeline_mode=pl.Buffered(1))],
        out_specs=[pl.BlockSpec((tm, ncols), lambda i: (i, 0)),
                   pl.BlockSpec((1, n_heads, V_DIM + ONES_ROWS, tm),
                                lambda i: (i // per_seq, 0, 0, i % per_seq))],
        out_shape=[jax.ShapeDtypeStruct((t, ncols), F32),
                   jax.ShapeDtypeStruct((b, n_heads, V_DIM + ONES_ROWS, s), BF16)],
        compiler_params=_cparams("arbitrary"),
        name="proj_vt",
    )(h, w)


def _lambda_full(lam_ref, lambda_init):
    lv = lam_ref[...]
    s1 = jnp.sum(lv[0:1] * lv[1:2], axis=-1, keepdims=True)
    s2 = jnp.sum(lv[2:3] * lv[3:4], axis=-1, keepdims=True)
    return jnp.exp(s1) - jnp.exp(s2) + lambda_init


AUX_SPLIT = 256


def _bf16_parts(x):
    a = x.astype(BF16).astype(F32)
    b = (x - a).astype(BF16).astype(F32)
    c = (x - a - b).astype(BF16).astype(F32)
    return a, b, c


def _alibi_aux(n, slope2, pos_side):
    col = lax.broadcasted_iota(jnp.int32, (n, HEAD_DIM), 1)
    pos = lax.broadcasted_iota(jnp.int32, (n, HEAD_DIM), 0)
    lo = jnp.bitwise_and(pos, AUX_SPLIT - 1)
    hi = pos - lo
    a, b, c = _bf16_parts(jnp.full((n, HEAD_DIM), slope2, F32))
    k3 = col - 3 * ((col >= 3).astype(jnp.int32) + (col >= 6).astype(jnp.int32)
                    + (col >= 9).astype(jnp.int32))
    slope_piece = jnp.where(k3 == 0, a, jnp.where(k3 == 1, b, c))
    first_half = (col < 3) | ((col >= 6) & (col < 9))
    pos_piece = jnp.where(first_half, hi, lo).astype(F32) * pos_side
    pos_cols = (col < 6) if pos_side > 0 else ((col >= 6) & (col < 12))
    slope_cols = ((col >= 6) & (col < 12)) if pos_side > 0 else (col < 6)
    return jnp.where(pos_cols, pos_piece, jnp.where(slope_cols, slope_piece, 0.0)).astype(BF16)


KIND_FULL = 0
KIND_LAST_FULL = 1
KIND_LAST_HALF = 2


def _attn_prompt_kernel(slopes_ref, qi_tab, j_tab, kind_tab, q_ref, k_ref, vt_ref, lam_ref, sw_ref,
                        o_ref, corr_ref, auxk_ref, auxq_ref, s_a, s_b, mx_a, mx_b, p_ref, m_ref,
                        acc_ref, *, tq, tk, nsteps, lambda_init):
    h = pl.program_id(1)
    slope2 = slopes_ref[h] * LOG2E

    c = lax.broadcasted_iota(jnp.int32, (tq, tq), 0)
    r = lax.broadcasted_iota(jnp.int32, (tq, tq), 1)
    allowed = jnp.right_shift(c, 6) <= jnp.right_shift(r, 6)
    corr_ref[...] = jnp.where(allowed, jnp.maximum(c - r, 0).astype(F32) * (-2.0 * slope2), NEG_BIG)
    auxk_ref[...] = _alibi_aux(tk, slope2, 1)
    auxq_ref[...] = _alibi_aux(tq, slope2, -1)
    acc_ref[...] = jnp.zeros_like(acc_ref)
    m_ref[...] = jnp.full_like(m_ref, NEG_BIG)

    def q_rows(blk):
        return pl.ds(pl.multiple_of(blk * tq, tq), tq)

    def k_rows(blk, n):
        return pl.ds(pl.multiple_of(blk * tk, tk), n)

    def kv_len(kind):
        return tq if kind == KIND_LAST_HALF else tk

    def scores(t, kind, s_out, mx_out):
        blk_q, blk_k = qi_tab[t], j_tab[t]
        n = kv_len(kind)
        for mp in range(2):
            cols = slice(mp * HEAD_DIM, (mp + 1) * HEAD_DIM)
            kk = jnp.concatenate([k_ref[0, k_rows(blk_k, n), cols], auxk_ref[:n, :]], axis=1)
            qq = jnp.concatenate([q_ref[0, q_rows(blk_q), cols], auxq_ref[...]], axis=1)
            s = _dot_nt(kk, qq)
            if kind == KIND_FULL:
                s_out[mp] = s
                mx_out[mp] = jnp.max(s, axis=0, keepdims=True)
            else:
                sq = s[n - tq:, :] + corr_ref[...]
                s_out[mp, n - tq:n, :] = sq
                mx = jnp.max(sq, axis=0, keepdims=True)
                if n > tq:
                    s_out[mp, :n - tq, :] = s[:n - tq, :]
                    mx = jnp.maximum(mx, jnp.max(s[:n - tq, :], axis=0, keepdims=True))
                mx_out[mp] = mx

    def consume(t, kind, s_in, mx_in):
        blk_q, blk_k = qi_tab[t], j_tab[t]
        n = kv_len(kind)
        cj = slope2 * (blk_q * tq - blk_k * tk).astype(F32)
        vj = vt_ref[0, 0, :, k_rows(blk_k, n)]
        for mp in range(2):
            m_old = jnp.where(blk_k == 0, NEG_BIG, m_ref[mp])
            m_new = jnp.maximum(m_old, mx_in[mp] - cj)
            m_ref[mp] = m_new
            p_ref[mp, :n, :] = jnp.exp2(s_in[mp, :n, :] - (m_new + cj)).astype(BF16)
            acc_ref[mp] = jnp.exp2(m_old - m_new) * acc_ref[mp] + _dot(vj, p_ref[mp, :n, :])

    def finish(blk_q):
        lam = _lambda_full(lam_ref, lambda_init)
        o1 = acc_ref[0, :V_DIM, :] * (1.0 / acc_ref[0, V_DIM:V_DIM + 1, :])
        o2 = acc_ref[1, :V_DIM, :] * (1.0 / acc_ref[1, V_DIM:V_DIM + 1, :])
        o = o1 - lam * o2
        on = o * lax.rsqrt(jnp.mean(o * o, axis=0, keepdims=True) + SUBLN_EPS)
        o_ref[0, q_rows(blk_q), :] = (on.T * sw_ref[...] * (1.0 - lambda_init)).astype(BF16)

    def step(t, s_cur, mx_cur, s_nxt, mx_nxt):
        cur_half = kind_tab[t] == KIND_LAST_HALF
        opens = (j_tab[t] == 0) & (t > 0)
        for nxt_kind in (KIND_FULL, KIND_LAST_FULL, KIND_LAST_HALF):
            for cur_kind, first in ((KIND_FULL, False), (KIND_FULL, True), (KIND_LAST_HALF, False)):
                @pl.when((kind_tab[t + 1] == nxt_kind) & (opens == first)
                         & (cur_half == (cur_kind == KIND_LAST_HALF)))
                def _():
                    scores(t + 1, nxt_kind, s_nxt, mx_nxt)
                    if first:
                        finish(qi_tab[t] - 1)
                    consume(t, cur_kind, s_cur, mx_cur)

    scores(0, KIND_LAST_HALF, s_a, mx_a)

    def pair(tt, carry):
        step(2 * tt, s_a, mx_a, s_b, mx_b)
        step(2 * tt + 1, s_b, mx_b, s_a, mx_a)
        return carry

    lax.fori_loop(0, nsteps // 2, pair, 0)
    if nsteps % 2:
        step(nsteps - 1, s_a, mx_a, s_b, mx_b)
    finish(qi_tab[nsteps - 1])


def _attn_prompt(q, k, vt, slopes, lam_vecs, subln_w, n_heads, lambda_init, tq):
    b, s, _ = q.shape
    tk = 2 * tq
    assert s % tk == 0
    nq = s // tq
    last_kind = (KIND_LAST_HALF, KIND_LAST_FULL)
    pairs = [(qi, j, last_kind[qi % 2] if j == qi // 2 else KIND_FULL)
             for qi in range(nq) for j in range(qi // 2 + 1)]
    pairs.append(pairs[-1])
    qi_tab, j_tab, kind_tab = (jnp.asarray(np.array([p[i] for p in pairs], np.int32))
                               for i in range(3))
    nsteps = len(pairs) - 1
    vrows = V_DIM + ONES_ROWS
    seq_spec = pl.BlockSpec((1, s, V_DIM), lambda bi, hi, *_: (bi, 0, hi))
    grid_spec = pltpu.PrefetchScalarGridSpec(
        num_scalar_prefetch=4,
        grid=(b, n_heads),
        in_specs=[seq_spec, seq_spec,
                  pl.BlockSpec((1, 1, vrows, s), lambda bi, hi, *_: (bi, hi, 0, 0)),
                  pl.BlockSpec((4, HEAD_DIM), lambda bi, hi, *_: (0, 0)),
                  pl.BlockSpec((1, V_DIM), lambda bi, hi, *_: (0, 0))],
        out_specs=seq_spec,
        scratch_shapes=[pltpu.VMEM((tq, tq), F32),
                        pltpu.VMEM((tk, HEAD_DIM), BF16),
                        pltpu.VMEM((tq, HEAD_DIM), BF16),
                        pltpu.VMEM((2, tk, tq), F32),
                        pltpu.VMEM((2, tk, tq), F32),
                        pltpu.VMEM((2, 1, tq), F32),
                        pltpu.VMEM((2, 1, tq), F32),
                        pltpu.VMEM((2, tk, tq), BF16),
                        pltpu.VMEM((2, 1, tq), F32),
                        pltpu.VMEM((2, vrows, tq), F32)],
    )
    return pl.pallas_call(
        functools.partial(_attn_prompt_kernel, tq=tq, tk=tk, nsteps=nsteps, lambda_init=lambda_init),
        grid_spec=grid_spec,
        out_shape=jax.ShapeDtypeStruct(q.shape, BF16),
        compiler_params=_cparams("arbitrary", "arbitrary"),
        name="attn_prompt",
    )(slopes, qi_tab, j_tab, kind_tab, q, k, vt, lam_vecs, subln_w.reshape(1, V_DIM))


def _attn_sample_kernel(slopes_ref, q_ref, kn_ref, vn_ref, kc_ref, vc_ref, lam_ref, sw_ref, o_ref,
                        *, past, n, slabs, lambda_init):
    h = pl.program_id(1)
    slope2 = slopes_ref[h] * LOG2E
    r_c = lax.broadcasted_iota(jnp.int32, (n, past), 0)
    c_c = lax.broadcasted_iota(jnp.int32, (n, past), 1)
    bias_c = (past + r_c - c_c).astype(F32) * (-slope2)
    r_n = lax.broadcasted_iota(jnp.int32, (n, n), 0)
    c_n = lax.broadcasted_iota(jnp.int32, (n, n), 1)
    bias_n = jnp.abs(r_n - c_n).astype(F32) * (-slope2)
    vc = vc_ref[0].astype(BF16)
    vn = vn_ref[0]
    outs = []
    for mp in range(2):
        sl = slice(mp * HEAD_DIM, (mp + 1) * HEAD_DIM)
        qm = q_ref[0, :, sl]
        kc = kc_ref[pl.ds(2 * h + mp, past, stride=slabs), :]
        s_c = _dot_nt(qm, kc.astype(BF16)) + bias_c
        s_n = _dot_nt(qm, kn_ref[0, :, sl]) + bias_n
        m = jnp.maximum(jnp.max(s_c, axis=-1, keepdims=True), jnp.max(s_n, axis=-1, keepdims=True))
        p_c = jnp.exp2(s_c - m)
        p_n = jnp.exp2(s_n - m)
        l = jnp.sum(p_c, axis=-1, keepdims=True) + jnp.sum(p_n, axis=-1, keepdims=True)
        acc = _dot(p_c.astype(BF16), vc) + _dot(p_n.astype(BF16), vn)
        outs.append(acc / l)
    lam = _lambda_full(lam_ref, lambda_init)
    o = outs[0] - lam * outs[1]
    o_ref[0] = (_rms(o, sw_ref[...], SUBLN_EPS) * (1.0 - lambda_init)).astype(BF16)


def _attn_sample(q, kn, vn, kc, vc, layer, slopes, lam_vecs, subln_w, n_heads, lambda_init):
    b, n, _ = q.shape
    past = vc.shape[1]
    slabs = 2 * n_heads
    new_spec = pl.BlockSpec((1, n, V_DIM), lambda bi, hi, sl: (bi, 0, hi))
    cache_spec = pl.BlockSpec((1, past, V_DIM), lambda bi, hi, sl: (layer * b + bi, 0, hi))
    kc_spec = pl.BlockSpec((past * slabs, HEAD_DIM), lambda bi, hi, sl: (layer * b + bi, 0))
    grid_spec = pltpu.PrefetchScalarGridSpec(
        num_scalar_prefetch=1,
        grid=(b, n_heads),
        in_specs=[new_spec, new_spec, new_spec, kc_spec, cache_spec,
                  pl.BlockSpec((4, HEAD_DIM), lambda bi, hi, sl: (0, 0)),
                  pl.BlockSpec((1, V_DIM), lambda bi, hi, sl: (0, 0))],
        out_specs=new_spec,
    )
    return pl.pallas_call(
        functools.partial(_attn_sample_kernel, past=past, n=n, slabs=slabs,
                          lambda_init=lambda_init),
        grid_spec=grid_spec,
        out_shape=jax.ShapeDtypeStruct(q.shape, BF16),
        compiler_params=_cparams("arbitrary", "arbitrary"),
        name="attn_sample",
    )(slopes, q, kn, vn, kc, vc, lam_vecs, subln_w.reshape(1, V_DIM))


def _gmlp_kernel(h_ref, wu_ref, wg_ref, lnw_ref, lnb_ref, ws_ref, bs_ref, *refs,
                 tm, width, rows_per_stream, want_vn):
    if want_vn:
        o_ref, vn_out_ref, g_ref = refs
    else:
        o_ref, g_ref = refs
    cb = 512
    gd = width // GM_GROUPS
    h = h_ref[...]
    rowsum = jnp.zeros((tm, 1), F32)
    for j in range(width // cb):
        g = _gelu(_dot(h, wg_ref[:, j * cb:(j + 1) * cb]))
        g_ref[:, j * cb:(j + 1) * cb] = g
        rowsum = rowsum + jnp.sum(g, axis=-1, keepdims=True)
    mean = rowsum / width
    sq = jnp.zeros((tm, 1), F32)
    for j in range(width // cb):
        xc = g_ref[:, j * cb:(j + 1) * cb] - mean
        sq = sq + jnp.sum(xc * xc, axis=-1, keepdims=True)
    rstd = lax.rsqrt(sq / width + NORM_EPS)

    r = lax.broadcasted_iota(jnp.int32, (GM_CHUNK, GM_CHUNK), 0)
    c = lax.broadcasted_iota(jnp.int32, (GM_CHUNK, GM_CHUNK), 1)
    causal = (c <= r) & ((c // rows_per_stream) == (r // rows_per_stream))
    for j in range(width // cb):
        cols = slice(j * cb, (j + 1) * cb)
        vn = (g_ref[:, cols] - mean) * rstd * lnw_ref[:, cols] + lnb_ref[:, cols]
        if want_vn:
            vn_out_ref[:, cols] = vn
        vnb = vn.astype(BF16)
        u = _gelu(_dot(h, wu_ref[:, cols]))
        for gg in range(cb // gd):
            grp = j * (cb // gd) + gg
            wsm = jnp.where(causal, ws_ref[grp], 0.0).astype(BF16)
            bias = bs_ref[grp]
            for ch in range(tm // GM_CHUNK):
                rows = slice(ch * GM_CHUNK, (ch + 1) * GM_CHUNK)
                mixed = _dot(wsm, vnb[rows, gg * gd:(gg + 1) * gd]) + bias
                o_ref[rows, j * cb + gg * gd:j * cb + (gg + 1) * gd] = (
                    u[rows, gg * gd:(gg + 1) * gd] * mixed).astype(BF16)


def _gmlp(h, w, col_u, col_g, lnw, lnb, ws_t, bs_t, tm, rows_per_stream, want_vn):
    t, k = h.shape
    width = lnw.shape[-1]
    out_shape = [jax.ShapeDtypeStruct((t, width), BF16)]
    out_specs = [pl.BlockSpec((tm, width), lambda i: (i, 0))]
    if want_vn:
        out_shape.append(jax.ShapeDtypeStruct((t, width), F32))
        out_specs.append(pl.BlockSpec((tm, width), lambda i: (i, 0)))
    one = pl.Buffered(1)
    res = pl.pallas_call(
        functools.partial(_gmlp_kernel, tm=tm, width=width, rows_per_stream=rows_per_stream,
                          want_vn=want_vn),
        grid=(t // tm,),
        in_specs=[pl.BlockSpec((tm, k), lambda i: (i, 0)),
                  pl.BlockSpec((k, width), lambda i: (0, col_u // width), pipeline_mode=one),
                  pl.BlockSpec((k, width), lambda i: (0, col_g // width), pipeline_mode=one),
                  pl.BlockSpec((1, width), lambda i: (0, 0)),
                  pl.BlockSpec((1, width), lambda i: (0, 0)),
                  pl.BlockSpec((GM_GROUPS, GM_CHUNK, GM_CHUNK), lambda i: (0, 0, 0)),
                  pl.BlockSpec((GM_GROUPS, GM_CHUNK, 1), lambda i: (0, 0, 0))],
        out_specs=out_specs,
        out_shape=out_shape,
        scratch_shapes=[pltpu.VMEM((tm, width), F32)],
        compiler_params=_cparams("arbitrary"),
        name="gmlp",
    )(h, w, w, lnw.reshape(1, width), lnb.reshape(1, width), ws_t, bs_t)
    return res


def _merge_kernel(h_ref, a_ref, g_ref, wa_ref, wb_ref, wba_ref, wbg_ref, o_ref):
    h = h_ref[...]
    ga = jax.nn.sigmoid(_dot(h, wa_ref[...]))
    gb = jax.nn.sigmoid(_dot(h, wb_ref[...]))
    o_ref[...] = (ga * _dot(a_ref[...], wba_ref[...]) + gb * _dot(g_ref[...], wbg_ref[...])).astype(BF16)


def _merge(h, attn, gm, w_in, col_a, col_b, wba, wbg, tm, tn):
    t, k = h.shape
    n = wba.shape[1]
    lhs = pl.BlockSpec((tm, k), lambda j, i: (i, 0))
    return pl.pallas_call(
        _merge_kernel,
        grid=(n // tn, t // tm),
        in_specs=[lhs, lhs, lhs,
                  pl.BlockSpec((k, tn), lambda j, i: (0, j + col_a // tn)),
                  pl.BlockSpec((k, tn), lambda j, i: (0, j + col_b // tn)),
                  pl.BlockSpec((k, tn), lambda j, i: (0, j)),
                  pl.BlockSpec((k, tn), lambda j, i: (0, j))],
        out_specs=pl.BlockSpec((tm, tn), lambda j, i: (i, j)),
        out_shape=jax.ShapeDtypeStruct((t, n), BF16),
        compiler_params=_cparams("arbitrary", "arbitrary"),
        name="merge",
    )(h, attn, gm, w_in, w_in, wba, wbg)


def _out_proj_kernel(m_ref, w_ref, x_ref, npost_ref, npre_ref, x1_ref, h2_ref):
    y = _dot(m_ref[...], w_ref[...])
    x1 = x_ref[...] + _rms(y, npost_ref[...], NORM_EPS)
    x1_ref[...] = x1
    h2_ref[...] = _rms(x1, npre_ref[...], NORM_EPS).astype(BF16)


def _out_proj(merged, w_out, x, npost, npre, tm):
    t, k = merged.shape
    d = w_out.shape[1]
    row = pl.BlockSpec((tm, d), lambda i: (i, 0))
    vec = pl.BlockSpec((1, d), lambda i: (0, 0))
    return pl.pallas_call(
        _out_proj_kernel,
        grid=(t // tm,),
        in_specs=[pl.BlockSpec((tm, k), lambda i: (i, 0)),
                  pl.BlockSpec((k, d), lambda i: (0, 0), pipeline_mode=pl.Buffered(1)),
                  row, vec, vec],
        out_specs=[row, row],
        out_shape=[jax.ShapeDtypeStruct((t, d), F32), jax.ShapeDtypeStruct((t, d), BF16)],
        compiler_params=_cparams("arbitrary"),
        name="out_proj",
    )(merged, w_out, x, npost.reshape(1, d), npre.reshape(1, d))


def _ffn_hidden_kernel(h_ref, wg_ref, wu_ref, o_ref):
    h = h_ref[...]
    o_ref[...] = (jax.nn.silu(_dot(h, wg_ref[...])) * _dot(h, wu_ref[...])).astype(BF16)


def _ffn_hidden(h2, wg, wu, tm, tn):
    t, k = h2.shape
    f = wg.shape[1]
    wspec = pl.BlockSpec((k, tn), lambda j, i: (0, j))
    return pl.pallas_call(
        _ffn_hidden_kernel,
        grid=(f // tn, t // tm),
        in_specs=[pl.BlockSpec((tm, k), lambda j, i: (i, 0)), wspec, wspec],
        out_specs=pl.BlockSpec((tm, tn), lambda j, i: (i, j)),
        out_shape=jax.ShapeDtypeStruct((t, f), BF16),
        compiler_params=_cparams("arbitrary", "arbitrary"),
        name="ffn_hidden",
    )(h2, wg, wu)


def _ffn_down_kernel(a_ref, w_ref, x1_ref, n_ref, o_ref):
    kk = pl.program_id(1)
    last = pl.num_programs(1) - 1

    @pl.when(kk == 0)
    def _():
        o_ref[...] = _dot(a_ref[...], w_ref[...])

    @pl.when((kk > 0) & (kk < last))
    def _():
        o_ref[...] += _dot(a_ref[...], w_ref[...])

    @pl.when(kk == last)
    def _():
        f = o_ref[...] + _dot(a_ref[...], w_ref[...])
        o_ref[...] = x1_ref[...] + _rms(f, n_ref[...], NORM_EPS)


def _ffn_down(hidden, wd, x1, npost, tm, tk):
    t, f = hidden.shape
    d = wd.shape[1]
    assert f // tk >= 2
    row = pl.BlockSpec((tm, d), lambda i, kk: (i, 0))
    return pl.pallas_call(
        _ffn_down_kernel,
        grid=(t // tm, f // tk),
        in_specs=[pl.BlockSpec((tm, tk), lambda i, kk: (i, kk)),
                  pl.BlockSpec((tk, d), lambda i, kk: (kk, 0)),
                  row,
                  pl.BlockSpec((1, d), lambda i, kk: (0, 0))],
        out_specs=row,
        out_shape=jax.ShapeDtypeStruct((t, d), F32),
        compiler_params=_cparams("arbitrary", "arbitrary"),
        name="ffn_down",
    )(hidden, wd, x1, npost.reshape(1, d))


def _largest_tile(t, cap):
    tm = min(t, cap)
    while t % tm:
        tm //= 2
    return tm


def _layer(x, p, lambda_init, cache):
    b, s, d = x.shape
    t = b * s
    n_heads = d // V_DIM
    x2 = x.reshape(t, d)
    tm_big = _largest_tile(t, 1024)
    tm_mid = _largest_tile(t, 512)

    qk_w = n_heads * 2 * HEAD_DIM
    col_k, col_v = qk_w, 2 * qk_w
    col_u = col_v + n_heads * V_DIM
    col_g = col_u + d
    col_a = col_g + d
    col_b = col_a + d
    w_in = p["w_in"]
    tn = 1024
    h, k_f32, k_bf = _norm_proj_k(x2, p["norm_mix_pre"], w_in, col_k, qk_w, tm_mid)
    (q,) = _proj(h, w_in, 0, qk_w, tm_mid, qk_w, scale=np.float32(HEAD_DIM ** -0.5 * LOG2E))

    slopes = 2.0 ** (-8.0 * jnp.arange(1, n_heads + 1, dtype=F32) / n_heads)
    lam_vecs = jnp.stack([p["lambda_q1"], p["lambda_k1"], p["lambda_q2"], p["lambda_k2"]])
    q3, k3 = q.reshape(b, s, -1), k_bf.reshape(b, s, -1)
    if cache is None:
        v_f32, vt = _proj_vt(h, w_in, col_v, b, s, n_heads, tm_mid)
        attn = _attn_prompt(q3, k3, vt, slopes, lam_vecs, p["subln_w"], n_heads, lambda_init,
                            _largest_tile(s, 512))
        rows_per_stream = GM_CHUNK
        ws_t = p["gm_ws"]
        bs_t = p["gm_bs"]
    else:
        ck, cv, layer = cache
        past = ck.shape[2]
        v_f32, v_bf = _proj(h, w_in, col_v, n_heads * V_DIM, tm_big, tn, want_f32=True)
        v3 = v_bf.reshape(b, s, -1)
        attn = _attn_sample(q3, k3, v3, ck.reshape(-1, HEAD_DIM), cv.reshape(-1, past, d), layer,
                            slopes, lam_vecs, p["subln_w"], n_heads, lambda_init)
        rows_per_stream = s
        reps = GM_CHUNK // s
        ws_t = jnp.tile(p["gm_ws"][:, :s, :s], (1, reps, reps))
        bs_t = jnp.tile(p["gm_bs"][:, :s], (1, reps))
    attn = attn.reshape(t, -1)

    res = _gmlp(h, w_in, col_u, col_g, p["gm_ln_w"], p["gm_ln_b"], ws_t,
                bs_t.reshape(GM_GROUPS, GM_CHUNK, 1), _largest_tile(t, 256), rows_per_stream,
                want_vn=cache is not None)
    gm = res[0]
    vn = res[1] if cache is not None else None

    merged = _merge(h, attn, gm, w_in, col_a, col_b, p["w_branch_attn"], p["w_branch_gmlp"],
                    tm_big, 512)
    x1, h2 = _out_proj(merged, p["w_out"], x2, p["norm_mix_post"], p["norm_ffn_pre"], tm_mid)
    hidden = _ffn_hidden(h2, p["w_ffn_gate"], p["w_ffn_up"], tm_big, 512)
    y = _ffn_down(hidden, p["w_ffn_down"], x1, p["norm_ffn_post"], tm_big, 512)
    return y.reshape(b, s, d), k_f32, v_f32, vn


def kernel(x_prompt, x_sample, cache_k, cache_v, norm_mix_pre, norm_mix_post, w_in, lambda_q1, lambda_k1, lambda_q2, lambda_k2, subln_w, gm_ln_w, gm_ln_b, gm_ws, gm_bs, w_branch_attn, w_branch_gmlp, w_out, norm_ffn_pre, norm_ffn_post, w_ffn_gate, w_ffn_up, w_ffn_down):
    depth = w_in.shape[0]
    bp, sp, d = x_prompt.shape
    bs_, ss, _ = x_sample.shape
    n_heads = d // V_DIM
    y_p, y_s = x_prompt, x_sample
    kp, vp, ks, vs, gs = [], [], [], [], []
    for l in range(depth):
        p = dict(norm_mix_pre=norm_mix_pre[l], norm_mix_post=norm_mix_post[l],
                 w_in=w_in[l].astype(BF16),
                 lambda_q1=lambda_q1[l], lambda_k1=lambda_k1[l], lambda_q2=lambda_q2[l],
                 lambda_k2=lambda_k2[l], subln_w=subln_w[l], gm_ln_w=gm_ln_w[l], gm_ln_b=gm_ln_b[l],
                 gm_ws=gm_ws[l], gm_bs=gm_bs[l],
                 w_branch_attn=w_branch_attn[l].astype(BF16),
                 w_branch_gmlp=w_branch_gmlp[l].astype(BF16),
                 w_out=w_out[l].astype(BF16),
                 norm_ffn_pre=norm_ffn_pre[l], norm_ffn_post=norm_ffn_post[l],
                 w_ffn_gate=w_ffn_gate[l].astype(BF16), w_ffn_up=w_ffn_up[l].astype(BF16),
                 w_ffn_down=w_ffn_down[l].astype(BF16))
        lambda_init = 0.8 - 0.6 * math.exp(-0.3 * l)
        y_p, k1, v1, _ = _layer(y_p, p, lambda_init, None)
        y_s, k2, v2, g2 = _layer(y_s, p, lambda_init, (cache_k, cache_v, l))
        kp.append(k1.reshape(bp, sp, n_heads, 2, HEAD_DIM))
        vp.append(v1.reshape(bp, sp, n_heads, V_DIM))
        ks.append(k2.reshape(bs_, ss, n_heads, 2, HEAD_DIM))
        vs.append(v2.reshape(bs_, ss, n_heads, V_DIM))
        gs.append(g2.reshape(bs_, ss, GM_GROUPS, d // GM_GROUPS))
    return (y_p, y_s, jnp.stack(kp), jnp.stack(vp), jnp.stack(ks), jnp.stack(vs), jnp.stack(gs))
```

```python
import functools
import math

import jax
import jax.numpy as jnp
import numpy as np
from jax import lax
from jax.experimental import pallas as pl
from jax.experimental.pallas import tpu as pltpu

CHUNK = 64
HEAD_DIM = 128
V_DIM = 2 * HEAD_DIM
GM_CHUNK = 128
GM_GROUPS = 8
NORM_EPS = 1e-6
SUBLN_EPS = 1e-5
LOG2E = math.log2(math.e)
NEG_BIG = -1e30

V7X_VMEM_BYTES = 64 * 1024 * 1024
VMEM_LIMIT_BYTES = 56 * 1024 * 1024

BF16 = jnp.bfloat16
F32 = jnp.float32


def _cparams(*sem, flags=None):
    return pltpu.CompilerParams(dimension_semantics=sem, vmem_limit_bytes=VMEM_LIMIT_BYTES,
                                flags=flags)


def _rms(xf, w, eps):
    return xf * lax.rsqrt(jnp.mean(xf * xf, axis=-1, keepdims=True) + eps) * w


def _gelu(x):
    return 0.5 * x * (1.0 + lax.erf(x * np.float32(np.sqrt(0.5))))


def _dot(a, b):
    return jnp.dot(a, b, preferred_element_type=F32)


def _dot_nt(a, b):
    return lax.dot_general(a, b, (((1,), (1,)), ((), ())), preferred_element_type=F32)


def _proj_kernel(h_ref, w_ref, *o_refs, scale, want_f32, want_bf16):
    z = _dot(h_ref[...], w_ref[...])
    if scale is not None:
        z = z * scale
    i = 0
    if want_f32:
        o_refs[i][...] = z
        i += 1
    if want_bf16:
        o_refs[i][...] = z.astype(BF16)


def _proj(h, w, col0, ncols, tm, tn, *, scale=None, want_f32=False, want_bf16=True):
    t, k = h.shape
    off = col0 // tn
    out_shape, out_specs = [], []
    if want_f32:
        out_shape.append(jax.ShapeDtypeStruct((t, ncols), F32))
        out_specs.append(pl.BlockSpec((tm, tn), lambda n, m: (m, n)))
    if want_bf16:
        out_shape.append(jax.ShapeDtypeStruct((t, ncols), BF16))
        out_specs.append(pl.BlockSpec((tm, tn), lambda n, m: (m, n)))
    return pl.pallas_call(
        functools.partial(_proj_kernel, scale=scale, want_f32=want_f32, want_bf16=want_bf16),
        grid=(ncols // tn, t // tm),
        in_specs=[pl.BlockSpec((tm, k), lambda n, m: (m, 0)),
                  pl.BlockSpec((k, tn), lambda n, m: (0, n + off))],
        out_specs=out_specs,
        out_shape=out_shape,
        compiler_params=_cparams("arbitrary", "arbitrary"),
        name="in_proj",
    )(h, w)


def _norm_proj_k_kernel(x_ref, nw_ref, w_ref, h_ref, kf_ref, kb_ref, *, tm, slabs):
    h = _rms(x_ref[...], nw_ref[...], NORM_EPS).astype(BF16)
    h_ref[...] = h
    z = _dot(h, w_ref[...])
    kb_ref[...] = z.astype(BF16)
    for c in range(slabs):
        kf_ref[pl.ds(c, tm, stride=slabs), :] = z[:, c * HEAD_DIM:(c + 1) * HEAD_DIM]


def _norm_proj_k(x, nw, w, col0, ncols, tm):
    t, d = x.shape
    slabs = ncols // HEAD_DIM
    row_d = pl.BlockSpec((tm, d), lambda i: (i, 0))
    row_n = pl.BlockSpec((tm, ncols), lambda i: (i, 0))
    return pl.pallas_call(
        functools.partial(_norm_proj_k_kernel, tm=tm, slabs=slabs),
        grid=(t // tm,),
        in_specs=[row_d,
                  pl.BlockSpec((1, d), lambda i: (0, 0)),
                  pl.BlockSpec((d, ncols), lambda i: (0, col0 // ncols),
                               pipeline_mode=pl.Buffered(1))],
        out_specs=[row_d, pl.BlockSpec((tm * slabs, HEAD_DIM), lambda i: (i, 0)), row_n],
        out_shape=[jax.ShapeDtypeStruct((t, d), BF16),
                   jax.ShapeDtypeStruct((t * slabs, HEAD_DIM), F32),
                   jax.ShapeDtypeStruct((t, ncols), BF16)],
        compiler_params=_cparams("arbitrary"),
        name="norm_proj_k",
    )(x, nw.reshape(1, d), w)


ONES_ROWS = 16


def _proj_vt_kernel(h_ref, w_ref, vf_ref, vt_ref, *, n_heads):
    z = _dot(h_ref[...], w_ref[...])
    vf_ref[...] = z
    tm = z.shape[0]
    for hh in range(n_heads):
        vt_ref[0, hh, :V_DIM, :] = z[:, hh * V_DIM:(hh + 1) * V_DIM].T.astype(BF16)
        vt_ref[0, hh, V_DIM:, :] = jnp.ones((ONES_ROWS, tm), BF16)


def _proj_vt(h, w, col0, b, s, n_heads, tm):
    t, k = h.shape
    ncols = n_heads * V_DIM
    per_seq = s // tm
    return pl.pallas_call(
        functools.partial(_proj_vt_kernel, n_heads=n_heads),
        grid=(t // tm,),
        in_specs=[pl.BlockSpec((tm, k), lambda i: (i, 0)),
                  pl.BlockSpec((k, ncols), lambda i: (0, col0 // ncols),
                               pipeline_mode=pl.Buffered(1))],
        out_specs=[pl.BlockSpec((tm, ncols), lambda i: (i, 0)),
                   pl.BlockSpec((1, n_heads, V_DIM + ONES_ROWS, tm),
                                lambda i: (i // per_seq, 0, 0, i % per_seq))],
        out_shape=[jax.ShapeDtypeStruct((t, ncols), F32),
                   jax.ShapeDtypeStruct((b, n_heads, V_DIM + ONES_ROWS, s), BF16)],
        compiler_params=_cparams("arbitrary"),
        name="proj_vt",
    )(h, w)


def _lambda_full(lam_ref, lambda_init):
    lv = lam_ref[...]
    s1 = jnp.sum(lv[0:1] * lv[1:2], axis=-1, keepdims=True)
    s2 = jnp.sum(lv[2:3] * lv[3:4], axis=-1, keepdims=True)
    return jnp.exp(s1) - jnp.exp(s2) + lambda_init


AUX_SPLIT = 256


def _bf16_parts(x):
    a = x.astype(BF16).astype(F32)
    b = (x - a).astype(BF16).astype(F32)
    c = (x - a - b).astype(BF16).astype(F32)
    return a, b, c


def _alibi_aux(n, slope2, pos_side):
    col = lax.broadcasted_iota(jnp.int32, (n, HEAD_DIM), 1)
    pos = lax.broadcasted_iota(jnp.int32, (n, HEAD_DIM), 0)
    lo = jnp.bitwise_and(pos, AUX_SPLIT - 1)
    hi = pos - lo
    a, b, c = _bf16_parts(jnp.full((n, HEAD_DIM), slope2, F32))
    k3 = col - 3 * ((col >= 3).astype(jnp.int32) + (col >= 6).astype(jnp.int32)
                    + (col >= 9).astype(jnp.int32))
    slope_piece = jnp.where(k3 == 0, a, jnp.where(k3 == 1, b, c))
    first_half = (col < 3) | ((col >= 6) & (col < 9))
    pos_piece = jnp.where(first_half, hi, lo).astype(F32) * pos_side
    pos_cols = (col < 6) if pos_side > 0 else ((col >= 6) & (col < 12))
    slope_cols = ((col >= 6) & (col < 12)) if pos_side > 0 else (col < 6)
    return jnp.where(pos_cols, pos_piece, jnp.where(slope_cols, slope_piece, 0.0)).astype(BF16)


KIND_FULL = 0
KIND_LAST_FULL = 1
KIND_LAST_HALF = 2


def _attn_prompt_kernel(slopes_ref, qi_tab, j_tab, kind_tab, q_ref, k_ref, vt_ref, lam_ref, sw_ref,
                        o_ref, corr_ref, auxk_ref, auxq_ref, s_a, s_b, mx_a, mx_b, p_ref, m_ref,
                        acc_ref, *, tq, tk, nsteps, lambda_init):
    h = pl.program_id(1)
    slope2 = slopes_ref[h] * LOG2E

    c = lax.broadcasted_iota(jnp.int32, (tq, tq), 0)
    r = lax.broadcasted_iota(jnp.int32, (tq, tq), 1)
    allowed = jnp.right_shift(c, 6) <= jnp.right_shift(r, 6)
    corr_ref[...] = jnp.where(allowed, jnp.maximum(c - r, 0).astype(F32) * (-2.0 * slope2), NEG_BIG)
    auxk_ref[...] = _alibi_aux(tk, slope2, 1)
    auxq_ref[...] = _alibi_aux(tq, slope2, -1)
    acc_ref[...] = jnp.zeros_like(acc_ref)
    m_ref[...] = jnp.full_like(m_ref, NEG_BIG)

    def q_rows(blk):
        return pl.ds(pl.multiple_of(blk * tq, tq), tq)

    def k_rows(blk, n):
        return pl.ds(pl.multiple_of(blk * tk, tk), n)

    def kv_len(kind):
        return tq if kind == KIND_LAST_HALF else tk

    def scores(t, kind, s_out, mx_out):
        blk_q, blk_k = qi_tab[t], j_tab[t]
        n = kv_len(kind)
        for mp in range(2):
            cols = slice(mp * HEAD_DIM, (mp + 1) * HEAD_DIM)
            kk = jnp.concatenate([k_ref[0, k_rows(blk_k, n), cols], auxk_ref[:n, :]], axis=1)
            qq = jnp.concatenate([q_ref[0, q_rows(blk_q), cols], auxq_ref[...]], axis=1)
            s = _dot_nt(kk, qq)
            if kind == KIND_FULL:
                s_out[mp] = s
                mx_out[mp] = jnp.max(s, axis=0, keepdims=True)
            else:
                sq = s[n - tq:, :] + corr_ref[...]
                s_out[mp, n - tq:n, :] = sq
                mx = jnp.max(sq, axis=0, keepdims=True)
                if n > tq:
                    s_out[mp, :n - tq, :] = s[:n - tq, :]
                    mx = jnp.maximum(mx, jnp.max(s[:n - tq, :], axis=0, keepdims=True))
                mx_out[mp] = mx

    def consume(t, kind, s_in, mx_in):
        blk_q, blk_k = qi_tab[t], j_tab[t]
        n = kv_len(kind)
        cj = slope2 * (blk_q * tq - blk_k * tk).astype(F32)
        vj = vt_ref[0, 0, :, k_rows(blk_k, n)]
        for mp in range(2):
            m_old = jnp.where(blk_k == 0, NEG_BIG, m_ref[mp])
            m_new = jnp.maximum(m_old, mx_in[mp] - cj)
            m_ref[mp] = m_new
            p_ref[mp, :n, :] = jnp.exp2(s_in[mp, :n, :] - (m_new + cj)).astype(BF16)
            acc_ref[mp] = jnp.exp2(m_old - m_new) * acc_ref[mp] + _dot(vj, p_ref[mp, :n, :])

    def finish(blk_q):
        lam = _lambda_full(lam_ref, lambda_init)
        o1 = acc_ref[0, :V_DIM, :] * (1.0 / acc_ref[0, V_DIM:V_DIM + 1, :])
        o2 = acc_ref[1, :V_DIM, :] * (1.0 / acc_ref[1, V_DIM:V_DIM + 1, :])
        o = o1 - lam * o2
        on = o * lax.rsqrt(jnp.mean(o * o, axis=0, keepdims=True) + SUBLN_EPS)
        o_ref[0, q_rows(blk_q), :] = (on.T * sw_ref[...] * (1.0 - lambda_init)).astype(BF16)

    def step(t, s_cur, mx_cur, s_nxt, mx_nxt):
        cur_half = kind_tab[t] == KIND_LAST_HALF
        opens = (j_tab[t] == 0) & (t > 0)
        cur_classes = ((KIND_FULL, False), (KIND_FULL, True), (KIND_LAST_HALF, False))
        cur_class = jnp.where(cur_half, 2, jnp.where(opens, 1, 0))

        def variant(nxt_kind, cur_kind, first):
            def run():
                scores(t + 1, nxt_kind, s_nxt, mx_nxt)
                if first:
                    finish(qi_tab[t] - 1)
                consume(t, cur_kind, s_cur, mx_cur)
            return run

        lax.switch(kind_tab[t + 1] * len(cur_classes) + cur_class,
                   [variant(nxt_kind, cur_kind, first)
                    for nxt_kind in (KIND_FULL, KIND_LAST_FULL, KIND_LAST_HALF)
                    for cur_kind, first in cur_classes])

    scores(0, KIND_LAST_HALF, s_a, mx_a)

    def pair(tt, carry):
        step(2 * tt, s_a, mx_a, s_b, mx_b)
        step(2 * tt + 1, s_b, mx_b, s_a, mx_a)
        return carry

    lax.fori_loop(0, nsteps // 2, pair, 0)
    if nsteps % 2:
        step(nsteps - 1, s_a, mx_a, s_b, mx_b)
    finish(qi_tab[nsteps - 1])


def _attn_prompt(q, k, vt, slopes, lam_vecs, subln_w, n_heads, lambda_init, tq):
    b, s, _ = q.shape
    tk = 2 * tq
    assert s % tk == 0
    nq = s // tq
    last_kind = (KIND_LAST_HALF, KIND_LAST_FULL)
    pairs = [(qi, j, last_kind[qi % 2] if j == qi // 2 else KIND_FULL)
             for qi in range(nq) for j in range(qi // 2 + 1)]
    pairs.append(pairs[-1])
    qi_tab, j_tab, kind_tab = (jnp.asarray(np.array([p[i] for p in pairs], np.int32))
                               for i in range(3))
    nsteps = len(pairs) - 1
    vrows = V_DIM + ONES_ROWS
    seq_spec = pl.BlockSpec((1, s, V_DIM), lambda bi, hi, *_: (bi, 0, hi))
    grid_spec = pltpu.PrefetchScalarGridSpec(
        num_scalar_prefetch=4,
        grid=(b, n_heads),
        in_specs=[seq_spec, seq_spec,
                  pl.BlockSpec((1, 1, vrows, s), lambda bi, hi, *_: (bi, hi, 0, 0)),
                  pl.BlockSpec((4, HEAD_DIM), lambda bi, hi, *_: (0, 0)),
                  pl.BlockSpec((1, V_DIM), lambda bi, hi, *_: (0, 0))],
        out_specs=seq_spec,
        scratch_shapes=[pltpu.VMEM((tq, tq), F32),
                        pltpu.VMEM((tk, HEAD_DIM), BF16),
                        pltpu.VMEM((tq, HEAD_DIM), BF16),
                        pltpu.VMEM((2, tk, tq), F32),
                        pltpu.VMEM((2, tk, tq), F32),
                        pltpu.VMEM((2, 1, tq), F32),
                        pltpu.VMEM((2, 1, tq), F32),
                        pltpu.VMEM((2, tk, tq), BF16),
                        pltpu.VMEM((2, 1, tq), F32),
                        pltpu.VMEM((2, vrows, tq), F32)],
    )
    return pl.pallas_call(
        functools.partial(_attn_prompt_kernel, tq=tq, tk=tk, nsteps=nsteps, lambda_init=lambda_init),
        grid_spec=grid_spec,
        out_shape=jax.ShapeDtypeStruct(q.shape, BF16),
        compiler_params=_cparams("arbitrary", "arbitrary"),
        name="attn_prompt",
    )(slopes, qi_tab, j_tab, kind_tab, q, k, vt, lam_vecs, subln_w.reshape(1, V_DIM))


def _attn_sample_kernel(slopes_ref, q_ref, kn_ref, vn_ref, kc_ref, vc_ref, lam_ref, sw_ref, o_ref,
                        *, past, n, slabs, lambda_init):
    h = pl.program_id(1)
    slope2 = slopes_ref[h] * LOG2E
    r_c = lax.broadcasted_iota(jnp.int32, (n, past), 0)
    c_c = lax.broadcasted_iota(jnp.int32, (n, past), 1)
    bias_c = (past + r_c - c_c).astype(F32) * (-slope2)
    r_n = lax.broadcasted_iota(jnp.int32, (n, n), 0)
    c_n = lax.broadcasted_iota(jnp.int32, (n, n), 1)
    bias_n = jnp.abs(r_n - c_n).astype(F32) * (-slope2)
    vc = vc_ref[0].astype(BF16)
    vn = vn_ref[0]
    outs = []
    for mp in range(2):
        sl = slice(mp * HEAD_DIM, (mp + 1) * HEAD_DIM)
        qm = q_ref[0, :, sl]
        kc = kc_ref[pl.ds(2 * h + mp, past, stride=slabs), :]
        s_c = _dot_nt(qm, kc.astype(BF16)) + bias_c
        s_n = _dot_nt(qm, kn_ref[0, :, sl]) + bias_n
        m = jnp.maximum(jnp.max(s_c, axis=-1, keepdims=True), jnp.max(s_n, axis=-1, keepdims=True))
        p_c = jnp.exp2(s_c - m)
        p_n = jnp.exp2(s_n - m)
        l = jnp.sum(p_c, axis=-1, keepdims=True) + jnp.sum(p_n, axis=-1, keepdims=True)
        acc = _dot(p_c.astype(BF16), vc) + _dot(p_n.astype(BF16), vn)
        outs.append(acc / l)
    lam = _lambda_full(lam_ref, lambda_init)
    o = outs[0] - lam * outs[1]
    o_ref[0] = (_rms(o, sw_ref[...], SUBLN_EPS) * (1.0 - lambda_init)).astype(BF16)


def _attn_sample(q, kn, vn, kc, vc, layer, slopes, lam_vecs, subln_w, n_heads, lambda_init):
    b, n, _ = q.shape
    past = vc.shape[1]
    slabs = 2 * n_heads
    new_spec = pl.BlockSpec((1, n, V_DIM), lambda bi, hi, sl: (bi, 0, hi))
    cache_spec = pl.BlockSpec((1, past, V_DIM), lambda bi, hi, sl: (layer * b + bi, 0, hi))
    kc_spec = pl.BlockSpec((past * slabs, HEAD_DIM), lambda bi, hi, sl: (layer * b + bi, 0))
    grid_spec = pltpu.PrefetchScalarGridSpec(
        num_scalar_prefetch=1,
        grid=(b, n_heads),
        in_specs=[new_spec, new_spec, new_spec, kc_spec, cache_spec,
                  pl.BlockSpec((4, HEAD_DIM), lambda bi, hi, sl: (0, 0)),
                  pl.BlockSpec((1, V_DIM), lambda bi, hi, sl: (0, 0))],
        out_specs=new_spec,
    )
    return pl.pallas_call(
        functools.partial(_attn_sample_kernel, past=past, n=n, slabs=slabs,
                          lambda_init=lambda_init),
        grid_spec=grid_spec,
        out_shape=jax.ShapeDtypeStruct(q.shape, BF16),
        compiler_params=_cparams("arbitrary", "arbitrary"),
        name="attn_sample",
    )(slopes, q, kn, vn, kc, vc, lam_vecs, subln_w.reshape(1, V_DIM))


def _gmlp_kernel(h_ref, wu_ref, wg_ref, lnw_ref, lnb_ref, ws_ref, bs_ref, *refs,
                 tm, width, rows_per_stream, want_vn):
    if want_vn:
        o_ref, vn_out_ref, g_ref = refs
    else:
        o_ref, g_ref = refs
    cb = 512
    gd = width // GM_GROUPS
    h = h_ref[...]
    rowsum = jnp.zeros((tm, 1), F32)
    for j in range(width // cb):
        g = _gelu(_dot(h, wg_ref[:, j * cb:(j + 1) * cb]))
        g_ref[:, j * cb:(j + 1) * cb] = g
        rowsum = rowsum + jnp.sum(g, axis=-1, keepdims=True)
    mean = rowsum / width
    sq = jnp.zeros((tm, 1), F32)
    for j in range(width // cb):
        xc = g_ref[:, j * cb:(j + 1) * cb] - mean
        sq = sq + jnp.sum(xc * xc, axis=-1, keepdims=True)
    rstd = lax.rsqrt(sq / width + NORM_EPS)

    r = lax.broadcasted_iota(jnp.int32, (GM_CHUNK, GM_CHUNK), 0)
    c = lax.broadcasted_iota(jnp.int32, (GM_CHUNK, GM_CHUNK), 1)
    causal = (c <= r) & ((c // rows_per_stream) == (r // rows_per_stream))
    for j in range(width // cb):
        cols = slice(j * cb, (j + 1) * cb)
        vn = (g_ref[:, cols] - mean) * rstd * lnw_ref[:, cols] + lnb_ref[:, cols]
        if want_vn:
            vn_out_ref[:, cols] = vn
        vnb = vn.astype(BF16)
        u = _gelu(_dot(h, wu_ref[:, cols]))
        for gg in range(cb // gd):
            grp = j * (cb // gd) + gg
            wsm = jnp.where(causal, ws_ref[grp], 0.0).astype(BF16)
            bias = bs_ref[grp]
            for ch in range(tm // GM_CHUNK):
                rows = slice(ch * GM_CHUNK, (ch + 1) * GM_CHUNK)
                mixed = _dot(wsm, vnb[rows, gg * gd:(gg + 1) * gd]) + bias
                o_ref[rows, j * cb + gg * gd:j * cb + (gg + 1) * gd] = (
                    u[rows, gg * gd:(gg + 1) * gd] * mixed).astype(BF16)


def _gmlp(h, w, col_u, col_g, lnw, lnb, ws_t, bs_t, tm, rows_per_stream, want_vn):
    t, k = h.shape
    width = lnw.shape[-1]
    out_shape = [jax.ShapeDtypeStruct((t, width), BF16)]
    out_specs = [pl.BlockSpec((tm, width), lambda i: (i, 0))]
    if want_vn:
        out_shape.append(jax.ShapeDtypeStruct((t, width), F32))
        out_specs.append(pl.BlockSpec((tm, width), lambda i: (i, 0)))
    one = pl.Buffered(1)
    res = pl.pallas_call(
        functools.partial(_gmlp_kernel, tm=tm, width=width, rows_per_stream=rows_per_stream,
                          want_vn=want_vn),
        grid=(t // tm,),
        in_specs=[pl.BlockSpec((tm, k), lambda i: (i, 0)),
                  pl.BlockSpec((k, width), lambda i: (0, col_u // width), pipeline_mode=one),
                  pl.BlockSpec((k, width), lambda i: (0, col_g // width), pipeline_mode=one),
                  pl.BlockSpec((1, width), lambda i: (0, 0)),
                  pl.BlockSpec((1, width), lambda i: (0, 0)),
                  pl.BlockSpec((GM_GROUPS, GM_CHUNK, GM_CHUNK), lambda i: (0, 0, 0)),
                  pl.BlockSpec((GM_GROUPS, GM_CHUNK, 1), lambda i: (0, 0, 0))],
        out_specs=out_specs,
        out_shape=out_shape,
        scratch_shapes=[pltpu.VMEM((tm, width), F32)],
        compiler_params=_cparams("arbitrary"),
        name="gmlp",
    )(h, w, w, lnw.reshape(1, width), lnb.reshape(1, width), ws_t, bs_t)
    return res


def _merge_kernel(h_ref, a_ref, g_ref, wa_ref, wb_ref, wba_ref, wbg_ref, o_ref):
    h = h_ref[...]
    ga = jax.nn.sigmoid(_dot(h, wa_ref[...]))
    gb = jax.nn.sigmoid(_dot(h, wb_ref[...]))
    o_ref[...] = (ga * _dot(a_ref[...], wba_ref[...]) + gb * _dot(g_ref[...], wbg_ref[...])).astype(BF16)


def _merge(h, attn, gm, w_in, col_a, col_b, wba, wbg, tm, tn):
    t, k = h.shape
    n = wba.shape[1]
    lhs = pl.BlockSpec((tm, k), lambda j, i: (i, 0))
    return pl.pallas_call(
        _merge_kernel,
        grid=(n // tn, t // tm),
        in_specs=[lhs, lhs, lhs,
                  pl.BlockSpec((k, tn), lambda j, i: (0, j + col_a // tn)),
                  pl.BlockSpec((k, tn), lambda j, i: (0, j + col_b // tn)),
                  pl.BlockSpec((k, tn), lambda j, i: (0, j)),
                  pl.BlockSpec((k, tn), lambda j, i: (0, j))],
        out_specs=pl.BlockSpec((tm, tn), lambda j, i: (i, j)),
        out_shape=jax.ShapeDtypeStruct((t, n), BF16),
        compiler_params=_cparams("arbitrary", "arbitrary"),
        name="merge",
    )(h, attn, gm, w_in, w_in, wba, wbg)


def _out_proj_kernel(m_ref, w_ref, x_ref, npost_ref, npre_ref, x1_ref, h2_ref):
    y = _dot(m_ref[...], w_ref[...])
    x1 = x_ref[...] + _rms(y, npost_ref[...], NORM_EPS)
    x1_ref[...] = x1
    h2_ref[...] = _rms(x1, npre_ref[...], NORM_EPS).astype(BF16)


def _out_proj(merged, w_out, x, npost, npre, tm):
    t, k = merged.shape
    d = w_out.shape[1]
    row = pl.BlockSpec((tm, d), lambda i: (i, 0))
    vec = pl.BlockSpec((1, d), lambda i: (0, 0))
    return pl.pallas_call(
        _out_proj_kernel,
        grid=(t // tm,),
        in_specs=[pl.BlockSpec((tm, k), lambda i: (i, 0)),
                  pl.BlockSpec((k, d), lambda i: (0, 0), pipeline_mode=pl.Buffered(1)),
                  row, vec, vec],
        out_specs=[row, row],
        out_shape=[jax.ShapeDtypeStruct((t, d), F32), jax.ShapeDtypeStruct((t, d), BF16)],
        compiler_params=_cparams("arbitrary"),
        name="out_proj",
    )(merged, w_out, x, npost.reshape(1, d), npre.reshape(1, d))


def _ffn_hidden_kernel(h_ref, wg_ref, wu_ref, o_ref):
    h = h_ref[...]
    o_ref[...] = (jax.nn.silu(_dot(h, wg_ref[...])) * _dot(h, wu_ref[...])).astype(BF16)


def _ffn_hidden(h2, wg, wu, tm, tn):
    t, k = h2.shape
    f = wg.shape[1]
    wspec = pl.BlockSpec((k, tn), lambda j, i: (0, j))
    return pl.pallas_call(
        _ffn_hidden_kernel,
        grid=(f // tn, t // tm),
        in_specs=[pl.BlockSpec((tm, k), lambda j, i: (i, 0)), wspec, wspec],
        out_specs=pl.BlockSpec((tm, tn), lambda j, i: (i, j)),
        out_shape=jax.ShapeDtypeStruct((t, f), BF16),
        compiler_params=_cparams("arbitrary", "arbitrary"),
        name="ffn_hidden",
    )(h2, wg, wu)


def _ffn_down_kernel(a_ref, w_ref, x1_ref, n_ref, o_ref):
    kk = pl.program_id(1)
    last = pl.num_programs(1) - 1

    @pl.when(kk == 0)
    def _():
        o_ref[...] = _dot(a_ref[...], w_ref[...])

    @pl.when((kk > 0) & (kk < last))
    def _():
        o_ref[...] += _dot(a_ref[...], w_ref[...])

    @pl.when(kk == last)
    def _():
        f = o_ref[...] + _dot(a_ref[...], w_ref[...])
        o_ref[...] = x1_ref[...] + _rms(f, n_ref[...], NORM_EPS)


def _ffn_down(hidden, wd, x1, npost, tm, tk):
    t, f = hidden.shape
    d = wd.shape[1]
    assert f // tk >= 2
    row = pl.BlockSpec((tm, d), lambda i, kk: (i, 0))
    return pl.pallas_call(
        _ffn_down_kernel,
        grid=(t // tm, f // tk),
        in_specs=[pl.BlockSpec((tm, tk), lambda i, kk: (i, kk)),
                  pl.BlockSpec((tk, d), lambda i, kk: (kk, 0)),
                  row,
                  pl.BlockSpec((1, d), lambda i, kk: (0, 0))],
        out_specs=row,
        out_shape=jax.ShapeDtypeStruct((t, d), F32),
        compiler_params=_cparams("arbitrary", "arbitrary"),
        name="ffn_down",
    )(hidden, wd, x1, npost.reshape(1, d))


def _largest_tile(t, cap):
    tm = min(t, cap)
    while t % tm:
        tm //= 2
    return tm


def _layer(x, p, lambda_init, cache):
    b, s, d = x.shape
    t = b * s
    n_heads = d // V_DIM
    x2 = x.reshape(t, d)
    tm_big = _largest_tile(t, 1024)
    tm_mid = _largest_tile(t, 512)

    qk_w = n_heads * 2 * HEAD_DIM
    col_k, col_v = qk_w, 2 * qk_w
    col_u = col_v + n_heads * V_DIM
    col_g = col_u + d
    col_a = col_g + d
    col_b = col_a + d
    w_in = p["w_in"]
    tn = 1024
    h, k_f32, k_bf = _norm_proj_k(x2, p["norm_mix_pre"], w_in, col_k, qk_w, tm_mid)
    (q,) = _proj(h, w_in, 0, qk_w, tm_mid, qk_w, scale=np.float32(HEAD_DIM ** -0.5 * LOG2E))

    slopes = 2.0 ** (-8.0 * jnp.arange(1, n_heads + 1, dtype=F32) / n_heads)
    lam_vecs = jnp.stack([p["lambda_q1"], p["lambda_k1"], p["lambda_q2"], p["lambda_k2"]])
    q3, k3 = q.reshape(b, s, -1), k_bf.reshape(b, s, -1)
    if cache is None:
        v_f32, vt = _proj_vt(h, w_in, col_v, b, s, n_heads, tm_mid)
        attn = _attn_prompt(q3, k3, vt, slopes, lam_vecs, p["subln_w"], n_heads, lambda_init,
                            _largest_tile(s, 512))
        rows_per_stream = GM_CHUNK
        ws_t = p["gm_ws"]
        bs_t = p["gm_bs"]
    else:
        ck, cv, layer = cache
        past = ck.shape[2]
        v_f32, v_bf = _proj(h, w_in, col_v, n_heads * V_DIM, tm_big, tn, want_f32=True)
        v3 = v_bf.reshape(b, s, -1)
        attn = _attn_sample(q3, k3, v3, ck.reshape(-1, HEAD_DIM), cv.reshape(-1, past, d), layer,
                            slopes, lam_vecs, p["subln_w"], n_heads, lambda_init)
        rows_per_stream = s
        reps = GM_CHUNK // s
        ws_t = jnp.tile(p["gm_ws"][:, :s, :s], (1, reps, reps))
        bs_t = jnp.tile(p["gm_bs"][:, :s], (1, reps))
    attn = attn.reshape(t, -1)

    res = _gmlp(h, w_in, col_u, col_g, p["gm_ln_w"], p["gm_ln_b"], ws_t,
                bs_t.reshape(GM_GROUPS, GM_CHUNK, 1), _largest_tile(t, 256), rows_per_stream,
                want_vn=cache is not None)
    gm = res[0]
    vn = res[1] if cache is not None else None

    merged = _merge(h, attn, gm, w_in, col_a, col_b, p["w_branch_attn"], p["w_branch_gmlp"],
                    tm_big, 512)
    x1, h2 = _out_proj(merged, p["w_out"], x2, p["norm_mix_post"], p["norm_ffn_pre"], tm_mid)
    hidden = _ffn_hidden(h2, p["w_ffn_gate"], p["w_ffn_up"], tm_big, 512)
    y = _ffn_down(hidden, p["w_ffn_down"], x1, p["norm_ffn_post"], tm_big, 512)
    return y.reshape(b, s, d), k_f32, v_f32, vn


def kernel(x_prompt, x_sample, cache_k, cache_v, norm_mix_pre, norm_mix_post, w_in, lambda_q1, lambda_k1, lambda_q2, lambda_k2, subln_w, gm_ln_w, gm_ln_b, gm_ws, gm_bs, w_branch_attn, w_branch_gmlp, w_out, norm_ffn_pre, norm_ffn_post, w_ffn_gate, w_ffn_up, w_ffn_down):
    depth = w_in.shape[0]
    bp, sp, d = x_prompt.shape
    bs_, ss, _ = x_sample.shape
    n_heads = d // V_DIM
    y_p, y_s = x_prompt, x_sample
    kp, vp, ks, vs, gs = [], [], [], [], []
    for l in range(depth):
        p = dict(norm_mix_pre=norm_mix_pre[l], norm_mix_post=norm_mix_post[l],
                 w_in=w_in[l].astype(BF16),
                 lambda_q1=lambda_q1[l], lambda_k1=lambda_k1[l], lambda_q2=lambda_q2[l],
                 lambda_k2=lambda_k2[l], subln_w=subln_w[l], gm_ln_w=gm_ln_w[l], gm_ln_b=gm_ln_b[l],
                 gm_ws=gm_ws[l], gm_bs=gm_bs[l],
                 w_branch_attn=w_branch_attn[l].astype(BF16),
                 w_branch_gmlp=w_branch_gmlp[l].astype(BF16),
                 w_out=w_out[l].astype(BF16),
                 norm_ffn_pre=norm_ffn_pre[l], norm_ffn_post=norm_ffn_post[l],
                 w_ffn_gate=w_ffn_gate[l].astype(BF16), w_ffn_up=w_ffn_up[l].astype(BF16),
                 w_ffn_down=w_ffn_down[l].astype(BF16))
        lambda_init = 0.8 - 0.6 * math.exp(-0.3 * l)
        y_p, k1, v1, _ = _layer(y_p, p, lambda_init, None)
        y_s, k2, v2, g2 = _layer(y_s, p, lambda_init, (cache_k, cache_v, l))
        kp.append(k1.reshape(bp, sp, n_heads, 2, HEAD_DIM))
        vp.append(v1.reshape(bp, sp, n_heads, V_DIM))
        ks.append(k2.reshape(bs_, ss, n_heads, 2, HEAD_DIM))
        vs.append(v2.reshape(bs_, ss, n_heads, V_DIM))
        gs.append(g2.reshape(bs_, ss, GM_GROUPS, d // GM_GROUPS))
    return (y_p, y_s, jnp.stack(kp), jnp.stack(vp), jnp.stack(ks), jnp.stack(vs), jnp.stack(gs))
```

```python
import functools
import math

import jax
import jax.numpy as jnp
import numpy as np
from jax import lax
from jax.experimental import pallas as pl
from jax.experimental.pallas import tpu as pltpu

CHUNK = 64
HEAD_DIM = 128
V_DIM = 2 * HEAD_DIM
GM_CHUNK = 128
GM_GROUPS = 8
NORM_EPS = 1e-6
SUBLN_EPS = 1e-5
LOG2E = math.log2(math.e)
NEG_BIG = -1e30

V7X_VMEM_BYTES = 64 * 1024 * 1024
VMEM_LIMIT_BYTES = 56 * 1024 * 1024

BF16 = jnp.bfloat16
F32 = jnp.float32


def _cparams(*sem, flags=None):
    return pltpu.CompilerParams(dimension_semantics=sem, vmem_limit_bytes=VMEM_LIMIT_BYTES,
                                flags=flags)


def _rms(xf, w, eps):
    return xf * lax.rsqrt(jnp.mean(xf * xf, axis=-1, keepdims=True) + eps) * w


def _gelu(x):
    return 0.5 * x * (1.0 + lax.erf(x * np.float32(np.sqrt(0.5))))


def _dot(a, b):
    return jnp.dot(a, b, preferred_element_type=F32)


def _dot_nt(a, b):
    return lax.dot_general(a, b, (((1,), (1,)), ((), ())), preferred_element_type=F32)


def _proj_kernel(h_ref, w_ref, *o_refs, scale, want_f32, want_bf16):
    z = _dot(h_ref[...], w_ref[...])
    if scale is not None:
        z = z * scale
    i = 0
    if want_f32:
        o_refs[i][...] = z
        i += 1
    if want_bf16:
        o_refs[i][...] = z.astype(BF16)


def _proj(h, w, col0, ncols, tm, tn, *, scale=None, want_f32=False, want_bf16=True):
    t, k = h.shape
    off = col0 // tn
    out_shape, out_specs = [], []
    if want_f32:
        out_shape.append(jax.ShapeDtypeStruct((t, ncols), F32))
        out_specs.append(pl.BlockSpec((tm, tn), lambda n, m: (m, n)))
    if want_bf16:
        out_shape.append(jax.ShapeDtypeStruct((t, ncols), BF16))
        out_specs.append(pl.BlockSpec((tm, tn), lambda n, m: (m, n)))
    return pl.pallas_call(
        functools.partial(_proj_kernel, scale=scale, want_f32=want_f32, want_bf16=want_bf16),
        grid=(ncols // tn, t // tm),
        in_specs=[pl.BlockSpec((tm, k), lambda n, m: (m, 0)),
                  pl.BlockSpec((k, tn), lambda n, m: (0, n + off))],
        out_specs=out_specs,
        out_shape=out_shape,
        compiler_params=_cparams("arbitrary", "arbitrary"),
        name="in_proj",
    )(h, w)


def _norm_proj_k_kernel(x_ref, nw_ref, w_ref, h_ref, kf_ref, kb_ref, *, tm, slabs):
    h = _rms(x_ref[...], nw_ref[...], NORM_EPS).astype(BF16)
    h_ref[...] = h
    z = _dot(h, w_ref[...])
    kb_ref[...] = z.astype(BF16)
    for c in range(slabs):
        kf_ref[pl.ds(c, tm, stride=slabs), :] = z[:, c * HEAD_DIM:(c + 1) * HEAD_DIM]


def _norm_proj_k(x, nw, w, col0, ncols, tm):
    t, d = x.shape
    slabs = ncols // HEAD_DIM
    row_d = pl.BlockSpec((tm, d), lambda i: (i, 0))
    row_n = pl.BlockSpec((tm, ncols), lambda i: (i, 0))
    return pl.pallas_call(
        functools.partial(_norm_proj_k_kernel, tm=tm, slabs=slabs),
        grid=(t // tm,),
        in_specs=[row_d,
                  pl.BlockSpec((1, d), lambda i: (0, 0)),
                  pl.BlockSpec((d, ncols), lambda i: (0, col0 // ncols),
                               pipeline_mode=pl.Buffered(1))],
        out_specs=[row_d, pl.BlockSpec((tm * slabs, HEAD_DIM), lambda i: (i, 0)), row_n],
        out_shape=[jax.ShapeDtypeStruct((t, d), BF16),
                   jax.ShapeDtypeStruct((t * slabs, HEAD_DIM), F32),
                   jax.ShapeDtypeStruct((t, ncols), BF16)],
        compiler_params=_cparams("arbitrary"),
        name="norm_proj_k",
    )(x, nw.reshape(1, d), w)


ONES_ROWS = 16


def _proj_vt_kernel(h_ref, w_ref, vf_ref, vt_ref, *, n_heads):
    z = _dot(h_ref[...], w_ref[...])
    vf_ref[...] = z
    tm = z.shape[0]
    for hh in range(n_heads):
        vt_ref[0, hh, :V_DIM, :] = z[:, hh * V_DIM:(hh + 1) * V_DIM].T.astype(BF16)
        vt_ref[0, hh, V_DIM:, :] = jnp.ones((ONES_ROWS, tm), BF16)


def _proj_vt(h, w, col0, b, s, n_heads, tm):
    t, k = h.shape
    ncols = n_heads * V_DIM
    per_seq = s // tm
    return pl.pallas_call(
        functools.partial(_proj_vt_kernel, n_heads=n_heads),
        grid=(t // tm,),
        in_specs=[pl.BlockSpec((tm, k), lambda i: (i, 0)),
                  pl.BlockSpec((k, ncols), lambda i: (0, col0 // ncols),
                               pipeline_mode=pl.Buffered(1))],
        out_specs=[pl.BlockSpec((tm, ncols), lambda i: (i, 0)),
                   pl.BlockSpec((1, n_heads, V_DIM + ONES_ROWS, tm),
                                lambda i: (i // per_seq, 0, 0, i % per_seq))],
        out_shape=[jax.ShapeDtypeStruct((t, ncols), F32),
                   jax.ShapeDtypeStruct((b, n_heads, V_DIM + ONES_ROWS, s), BF16)],
        compiler_params=_cparams("arbitrary"),
        name="proj_vt",
    )(h, w)


def _lambda_full(lam_ref, lambda_init):
    lv = lam_ref[...]
    s1 = jnp.sum(lv[0:1] * lv[1:2], axis=-1, keepdims=True)
    s2 = jnp.sum(lv[2:3] * lv[3:4], axis=-1, keepdims=True)
    return jnp.exp(s1) - jnp.exp(s2) + lambda_init


AUX_SPLIT = 256


def _bf16_parts(x):
    a = x.astype(BF16).astype(F32)
    b = (x - a).astype(BF16).astype(F32)
    c = (x - a - b).astype(BF16).astype(F32)
    return a, b, c


def _alibi_aux(n, slope2, pos_side):
    col = lax.broadcasted_iota(jnp.int32, (n, HEAD_DIM), 1)
    pos = lax.broadcasted_iota(jnp.int32, (n, HEAD_DIM), 0)
    lo = jnp.bitwise_and(pos, AUX_SPLIT - 1)
    hi = pos - lo
    a, b, c = _bf16_parts(jnp.full((n, HEAD_DIM), slope2, F32))
    k3 = col - 3 * ((col >= 3).astype(jnp.int32) + (col >= 6).astype(jnp.int32)
                    + (col >= 9).astype(jnp.int32))
    slope_piece = jnp.where(k3 == 0, a, jnp.where(k3 == 1, b, c))
    first_half = (col < 3) | ((col >= 6) & (col < 9))
    pos_piece = jnp.where(first_half, hi, lo).astype(F32) * pos_side
    pos_cols = (col < 6) if pos_side > 0 else ((col >= 6) & (col < 12))
    slope_cols = ((col >= 6) & (col < 12)) if pos_side > 0 else (col < 6)
    return jnp.where(pos_cols, pos_piece, jnp.where(slope_cols, slope_piece, 0.0)).astype(BF16)


KIND_FULL = 0
KIND_LAST_FULL = 1
KIND_LAST_HALF = 2


def _attn_prompt_kernel(slopes_ref, qi_tab, j_tab, kind_tab, q_ref, k_ref, vt_ref, lam_ref, sw_ref,
                        o_ref, corr_ref, auxk_ref, auxq_ref, s_a, s_b, mx_a, mx_b, p_ref, m_ref,
                        acc_ref, *, tq, tk, nsteps, lambda_init):
    h = pl.program_id(1)
    slope2 = slopes_ref[h] * LOG2E

    c = lax.broadcasted_iota(jnp.int32, (tq, tq), 0)
    r = lax.broadcasted_iota(jnp.int32, (tq, tq), 1)
    allowed = jnp.right_shift(c, 6) <= jnp.right_shift(r, 6)
    corr_ref[...] = jnp.where(allowed, jnp.maximum(c - r, 0).astype(F32) * (-2.0 * slope2), NEG_BIG)
    auxk_ref[...] = _alibi_aux(tk, slope2, 1)
    auxq_ref[...] = _alibi_aux(tq, slope2, -1)
    acc_ref[...] = jnp.zeros_like(acc_ref)
    m_ref[...] = jnp.full_like(m_ref, NEG_BIG)

    def q_rows(blk):
        return pl.ds(pl.multiple_of(blk * tq, tq), tq)

    def k_rows(blk, n):
        return pl.ds(pl.multiple_of(blk * tk, tk), n)

    def kv_len(kind):
        return tq if kind == KIND_LAST_HALF else tk

    def scores(t, kind, s_out, mx_out):
        blk_q, blk_k = qi_tab[t], j_tab[t]
        n = kv_len(kind)
        for mp in range(2):
            cols = slice(mp * HEAD_DIM, (mp + 1) * HEAD_DIM)
            kk = jnp.concatenate([k_ref[0, k_rows(blk_k, n), cols], auxk_ref[:n, :]], axis=1)
            qq = jnp.concatenate([q_ref[0, q_rows(blk_q), cols], auxq_ref[...]], axis=1)
            s = _dot_nt(kk, qq)
            if kind == KIND_FULL:
                s_out[mp] = s
                mx_out[mp] = jnp.max(s, axis=0, keepdims=True)
            else:
                sq = s[n - tq:, :] + corr_ref[...]
                s_out[mp, n - tq:n, :] = sq
                mx = jnp.max(sq, axis=0, keepdims=True)
                if n > tq:
                    s_out[mp, :n - tq, :] = s[:n - tq, :]
                    mx = jnp.maximum(mx, jnp.max(s[:n - tq, :], axis=0, keepdims=True))
                mx_out[mp] = mx

    def consume(t, kind, s_in, mx_in):
        blk_q, blk_k = qi_tab[t], j_tab[t]
        n = kv_len(kind)
        cj = slope2 * (blk_q * tq - blk_k * tk).astype(F32)
        vj = vt_ref[0, 0, :, k_rows(blk_k, n)]
        for mp in range(2):
            m_old = jnp.where(blk_k == 0, NEG_BIG, m_ref[mp])
            m_new = jnp.maximum(m_old, mx_in[mp] - cj)
            m_ref[mp] = m_new
            p_ref[mp, :n, :] = jnp.exp2(s_in[mp, :n, :] - (m_new + cj)).astype(BF16)
            acc_ref[mp] = jnp.exp2(m_old - m_new) * acc_ref[mp] + _dot(vj, p_ref[mp, :n, :])

    def finish(blk_q):
        lam = _lambda_full(lam_ref, lambda_init)
        o1 = acc_ref[0, :V_DIM, :] * (1.0 / acc_ref[0, V_DIM:V_DIM + 1, :])
        o2 = acc_ref[1, :V_DIM, :] * (1.0 / acc_ref[1, V_DIM:V_DIM + 1, :])
        o = o1 - lam * o2
        on = o * lax.rsqrt(jnp.mean(o * o, axis=0, keepdims=True) + SUBLN_EPS)
        o_ref[0, q_rows(blk_q), :] = (on.T * sw_ref[...] * (1.0 - lambda_init)).astype(BF16)

    def step(t, s_cur, mx_cur, s_nxt, mx_nxt):
        cur_half = kind_tab[t] == KIND_LAST_HALF
        opens = (j_tab[t] == 0) & (t > 0)
        cur_classes = ((KIND_FULL, False), (KIND_FULL, True), (KIND_LAST_HALF, False))
        cur_class = jnp.where(cur_half, 2, jnp.where(opens, 1, 0))

        def variant(nxt_kind, cur_kind, first):
            def run():
                scores(t + 1, nxt_kind, s_nxt, mx_nxt)
                if first:
                    finish(qi_tab[t] - 1)
                consume(t, cur_kind, s_cur, mx_cur)
            return run

        lax.switch(kind_tab[t + 1] * len(cur_classes) + cur_class,
                   [variant(nxt_kind, cur_kind, first)
                    for nxt_kind in (KIND_FULL, KIND_LAST_FULL, KIND_LAST_HALF)
                    for cur_kind, first in cur_classes])

    scores(0, KIND_LAST_HALF, s_a, mx_a)

    def pair(tt, carry):
        step(2 * tt, s_a, mx_a, s_b, mx_b)
        step(2 * tt + 1, s_b, mx_b, s_a, mx_a)
        return carry

    lax.fori_loop(0, nsteps // 2, pair, 0)
    if nsteps % 2:
        step(nsteps - 1, s_a, mx_a, s_b, mx_b)
    finish(qi_tab[nsteps - 1])


def _attn_prompt(q, k, vt, slopes, lam_vecs, subln_w, n_heads, lambda_init, tq):
    b, s, _ = q.shape
    tk = 2 * tq
    assert s % tk == 0
    nq = s // tq
    last_kind = (KIND_LAST_HALF, KIND_LAST_FULL)
    pairs = [(qi, j, last_kind[qi % 2] if j == qi // 2 else KIND_FULL)
             for qi in range(nq) for j in range(qi // 2 + 1)]
    pairs.append(pairs[-1])
    qi_tab, j_tab, kind_tab = (jnp.asarray(np.array([p[i] for p in pairs], np.int32))
                               for i in range(3))
    nsteps = len(pairs) - 1
    vrows = V_DIM + ONES_ROWS
    seq_spec = pl.BlockSpec((1, s, V_DIM), lambda bi, hi, *_: (bi, 0, hi))
    grid_spec = pltpu.PrefetchScalarGridSpec(
        num_scalar_prefetch=4,
        grid=(b, n_heads),
        in_specs=[seq_spec, seq_spec,
                  pl.BlockSpec((1, 1, vrows, s), lambda bi, hi, *_: (bi, hi, 0, 0)),
                  pl.BlockSpec((4, HEAD_DIM), lambda bi, hi, *_: (0, 0)),
                  pl.BlockSpec((1, V_DIM), lambda bi, hi, *_: (0, 0))],
        out_specs=seq_spec,
        scratch_shapes=[pltpu.VMEM((tq, tq), F32),
                        pltpu.VMEM((tk, HEAD_DIM), BF16),
                        pltpu.VMEM((tq, HEAD_DIM), BF16),
                        pltpu.VMEM((2, tk, tq), F32),
                        pltpu.VMEM((2, tk, tq), F32),
                        pltpu.VMEM((2, 1, tq), F32),
                        pltpu.VMEM((2, 1, tq), F32),
                        pltpu.VMEM((2, tk, tq), BF16),
                        pltpu.VMEM((2, 1, tq), F32),
                        pltpu.VMEM((2, vrows, tq), F32)],
    )
    return pl.pallas_call(
        functools.partial(_attn_prompt_kernel, tq=tq, tk=tk, nsteps=nsteps, lambda_init=lambda_init),
        grid_spec=grid_spec,
        out_shape=jax.ShapeDtypeStruct(q.shape, BF16),
        compiler_params=_cparams("arbitrary", "arbitrary"),
        name="attn_prompt",
    )(slopes, qi_tab, j_tab, kind_tab, q, k, vt, lam_vecs, subln_w.reshape(1, V_DIM))


def _attn_sample_kernel(slopes_ref, q_ref, kn_ref, vn_ref, kc_ref, vc_ref, lam_ref, sw_ref, o_ref,
                        *, past, n, slabs, lambda_init):
    h = pl.program_id(1)
    slope2 = slopes_ref[h] * LOG2E
    r_c = lax.broadcasted_iota(jnp.int32, (n, past), 0)
    c_c = lax.broadcasted_iota(jnp.int32, (n, past), 1)
    bias_c = (past + r_c - c_c).astype(F32) * (-slope2)
    r_n = lax.broadcasted_iota(jnp.int32, (n, n), 0)
    c_n = lax.broadcasted_iota(jnp.int32, (n, n), 1)
    bias_n = jnp.abs(r_n - c_n).astype(F32) * (-slope2)
    vc = vc_ref[0].astype(BF16)
    vn = vn_ref[0]
    outs = []
    for mp in range(2):
        sl = slice(mp * HEAD_DIM, (mp + 1) * HEAD_DIM)
        qm = q_ref[0, :, sl]
        kc = kc_ref[pl.ds(2 * h + mp, past, stride=slabs), :]
        s_c = _dot_nt(qm, kc.astype(BF16)) + bias_c
        s_n = _dot_nt(qm, kn_ref[0, :, sl]) + bias_n
        m = jnp.maximum(jnp.max(s_c, axis=-1, keepdims=True), jnp.max(s_n, axis=-1, keepdims=True))
        p_c = jnp.exp2(s_c - m)
        p_n = jnp.exp2(s_n - m)
        l = jnp.sum(p_c, axis=-1, keepdims=True) + jnp.sum(p_n, axis=-1, keepdims=True)
        acc = _dot(p_c.astype(BF16), vc) + _dot(p_n.astype(BF16), vn)
        outs.append(acc / l)
    lam = _lambda_full(lam_ref, lambda_init)
    o = outs[0] - lam * outs[1]
    o_ref[0] = (_rms(o, sw_ref[...], SUBLN_EPS) * (1.0 - lambda_init)).astype(BF16)


def _attn_sample(q, kn, vn, kc, vc, layer, slopes, lam_vecs, subln_w, n_heads, lambda_init):
    b, n, _ = q.shape
    past = vc.shape[1]
    slabs = 2 * n_heads
    new_spec = pl.BlockSpec((1, n, V_DIM), lambda bi, hi, sl: (bi, 0, hi))
    cache_spec = pl.BlockSpec((1, past, V_DIM), lambda bi, hi, sl: (layer * b + bi, 0, hi))
    kc_spec = pl.BlockSpec((past * slabs, HEAD_DIM), lambda bi, hi, sl: (layer * b + bi, 0))
    grid_spec = pltpu.PrefetchScalarGridSpec(
        num_scalar_prefetch=1,
        grid=(b, n_heads),
        in_specs=[new_spec, new_spec, new_spec, kc_spec, cache_spec,
                  pl.BlockSpec((4, HEAD_DIM), lambda bi, hi, sl: (0, 0)),
                  pl.BlockSpec((1, V_DIM), lambda bi, hi, sl: (0, 0))],
        out_specs=new_spec,
    )
    return pl.pallas_call(
        functools.partial(_attn_sample_kernel, past=past, n=n, slabs=slabs,
                          lambda_init=lambda_init),
        grid_spec=grid_spec,
        out_shape=jax.ShapeDtypeStruct(q.shape, BF16),
        compiler_params=_cparams("arbitrary", "arbitrary"),
        name="attn_sample",
    )(slopes, q, kn, vn, kc, vc, lam_vecs, subln_w.reshape(1, V_DIM))


def _gmlp_kernel(h_ref, wu_ref, wg_ref, lnw_ref, lnb_ref, ws_ref, bs_ref, *refs,
                 tm, width, rows_per_stream, want_vn):
    if want_vn:
        o_ref, vn_out_ref, g_ref = refs
    else:
        o_ref, g_ref = refs
    cb = 512
    gd = width // GM_GROUPS
    h = h_ref[...]
    rowsum = jnp.zeros((tm, 1), F32)
    for j in range(width // cb):
        g = _gelu(_dot(h, wg_ref[:, j * cb:(j + 1) * cb]))
        g_ref[:, j * cb:(j + 1) * cb] = g
        rowsum = rowsum + jnp.sum(g, axis=-1, keepdims=True)
    mean = rowsum / width
    sq = jnp.zeros((tm, 1), F32)
    for j in range(width // cb):
        xc = g_ref[:, j * cb:(j + 1) * cb] - mean
        sq = sq + jnp.sum(xc * xc, axis=-1, keepdims=True)
    rstd = lax.rsqrt(sq / width + NORM_EPS)

    r = lax.broadcasted_iota(jnp.int32, (GM_CHUNK, GM_CHUNK), 0)
    c = lax.broadcasted_iota(jnp.int32, (GM_CHUNK, GM_CHUNK), 1)
    causal = (c <= r) & ((c // rows_per_stream) == (r // rows_per_stream))
    for j in range(width // cb):
        cols = slice(j * cb, (j + 1) * cb)
        vn = (g_ref[:, cols] - mean) * rstd * lnw_ref[:, cols] + lnb_ref[:, cols]
        if want_vn:
            vn_out_ref[:, cols] = vn
        vnb = vn.astype(BF16)
        u = _gelu(_dot(h, wu_ref[:, cols]))
        for gg in range(cb // gd):
            grp = j * (cb // gd) + gg
            wsm = jnp.where(causal, ws_ref[grp], 0.0).astype(BF16)
            bias = bs_ref[grp]
            for ch in range(tm // GM_CHUNK):
                rows = slice(ch * GM_CHUNK, (ch + 1) * GM_CHUNK)
                mixed = _dot(wsm, vnb[rows, gg * gd:(gg + 1) * gd]) + bias
                o_ref[rows, j * cb + gg * gd:j * cb + (gg + 1) * gd] = (
                    u[rows, gg * gd:(gg + 1) * gd] * mixed).astype(BF16)


def _gmlp(h, w, col_u, col_g, lnw, lnb, ws_t, bs_t, tm, rows_per_stream, want_vn):
    t, k = h.shape
    width = lnw.shape[-1]
    out_shape = [jax.ShapeDtypeStruct((t, width), BF16)]
    out_specs = [pl.BlockSpec((tm, width), lambda i: (i, 0))]
    if want_vn:
        out_shape.append(jax.ShapeDtypeStruct((t, width), F32))
        out_specs.append(pl.BlockSpec((tm, width), lambda i: (i, 0)))
    one = pl.Buffered(1)
    res = pl.pallas_call(
        functools.partial(_gmlp_kernel, tm=tm, width=width, rows_per_stream=rows_per_stream,
                          want_vn=want_vn),
        grid=(t // tm,),
        in_specs=[pl.BlockSpec((tm, k), lambda i: (i, 0)),
                  pl.BlockSpec((k, width), lambda i: (0, col_u // width), pipeline_mode=one),
                  pl.BlockSpec((k, width), lambda i: (0, col_g // width), pipeline_mode=one),
                  pl.BlockSpec((1, width), lambda i: (0, 0)),
                  pl.BlockSpec((1, width), lambda i: (0, 0)),
                  pl.BlockSpec((GM_GROUPS, GM_CHUNK, GM_CHUNK), lambda i: (0, 0, 0)),
                  pl.BlockSpec((GM_GROUPS, GM_CHUNK, 1), lambda i: (0, 0, 0))],
        out_specs=out_specs,
        out_shape=out_shape,
        scratch_shapes=[pltpu.VMEM((tm, width), F32)],
        compiler_params=_cparams("arbitrary"),
        name="gmlp",
    )(h, w, w, lnw.reshape(1, width), lnb.reshape(1, width), ws_t, bs_t)
    return res


def _merge_kernel(h_ref, a_ref, g_ref, wa_ref, wb_ref, wba_ref, wbg_ref, o_ref):
    h = h_ref[...]
    ga = jax.nn.sigmoid(_dot(h, wa_ref[...]))
    gb = jax.nn.sigmoid(_dot(h, wb_ref[...]))
    o_ref[...] = (ga * _dot(a_ref[...], wba_ref[...]) + gb * _dot(g_ref[...], wbg_ref[...])).astype(BF16)


def _merge(h, attn, gm, w_in, col_a, col_b, wba, wbg, tm, tn):
    t, k = h.shape
    n = wba.shape[1]
    lhs = pl.BlockSpec((tm, k), lambda j, i: (i, 0))
    return pl.pallas_call(
        _merge_kernel,
        grid=(n // tn, t // tm),
        in_specs=[lhs, lhs, lhs,
                  pl.BlockSpec((k, tn), lambda j, i: (0, j + col_a // tn)),
                  pl.BlockSpec((k, tn), lambda j, i: (0, j + col_b // tn)),
                  pl.BlockSpec((k, tn), lambda j, i: (0, j)),
                  pl.BlockSpec((k, tn), lambda j, i: (0, j))],
        out_specs=pl.BlockSpec((tm, tn), lambda j, i: (i, j)),
        out_shape=jax.ShapeDtypeStruct((t, n), BF16),
        compiler_params=_cparams("arbitrary", "arbitrary"),
        name="merge",
    )(h, attn, gm, w_in, w_in, wba, wbg)


def _out_proj_kernel(m_ref, w_ref, x_ref, npost_ref, npre_ref, x1_ref, h2_ref):
    y = _dot(m_ref[...], w_ref[...])
    x1 = x_ref[...] + _rms(y, npost_ref[...], NORM_EPS)
    x1_ref[...] = x1
    h2_ref[...] = _rms(x1, npre_ref[...], NORM_EPS).astype(BF16)


def _out_proj(merged, w_out, x, npost, npre, tm):
    t, k = merged.shape
    d = w_out.shape[1]
    row = pl.BlockSpec((tm, d), lambda i: (i, 0))
    vec = pl.BlockSpec((1, d), lambda i: (0, 0))
    return pl.pallas_call(
        _out_proj_kernel,
        grid=(t // tm,),
        in_specs=[pl.BlockSpec((tm, k), lambda i: (i, 0)),
                  pl.BlockSpec((k, d), lambda i: (0, 0), pipeline_mode=pl.Buffered(1)),
                  row, vec, vec],
        out_specs=[row, row],
        out_shape=[jax.ShapeDtypeStruct((t, d), F32), jax.ShapeDtypeStruct((t, d), BF16)],
        compiler_params=_cparams("arbitrary"),
        name="out_proj",
    )(merged, w_out, x, npost.reshape(1, d), npre.reshape(1, d))


def _ffn_hidden_kernel(h_ref, wg_ref, wu_ref, o_ref):
    h = h_ref[...]
    o_ref[...] = (jax.nn.silu(_dot(h, wg_ref[...])) * _dot(h, wu_ref[...])).astype(BF16)


def _ffn_hidden(h2, wg, wu, tm, tn):
    t, k = h2.shape
    f = wg.shape[1]
    wspec = pl.BlockSpec((k, tn), lambda j, i: (0, j))
    return pl.pallas_call(
        _ffn_hidden_kernel,
        grid=(f // tn, t // tm),
        in_specs=[pl.BlockSpec((tm, k), lambda j, i: (i, 0)), wspec, wspec],
        out_specs=pl.BlockSpec((tm, tn), lambda j, i: (i, j)),
        out_shape=jax.ShapeDtypeStruct((t, f), BF16),
        compiler_params=_cparams("arbitrary", "arbitrary"),
        name="ffn_hidden",
    )(h2, wg, wu)


def _ffn_down_kernel(a_ref, w_ref, x1_ref, n_ref, o_ref):
    kk = pl.program_id(1)
    last = pl.num_programs(1) - 1
    half = o_ref.shape[1] // 2

    def middle():
        for c in range(2):
            cols = slice(c * half, (c + 1) * half)
            o_ref[:, cols] += _dot(a_ref[...], w_ref[:, cols])

    def first():
        for c in range(2):
            cols = slice(c * half, (c + 1) * half)
            o_ref[:, cols] = _dot(a_ref[...], w_ref[:, cols])

    def final():
        f = o_ref[...] + _dot(a_ref[...], w_ref[...])
        o_ref[...] = x1_ref[...] + _rms(f, n_ref[...], NORM_EPS)

    lax.switch(jnp.where(kk == 0, 1, jnp.where(kk == last, 2, 0)), [middle, first, final])


def _ffn_down(hidden, wd, x1, npost, tm, tk):
    t, f = hidden.shape
    d = wd.shape[1]
    assert f // tk >= 2
    row = pl.BlockSpec((tm, d), lambda i, kk: (i, 0))
    return pl.pallas_call(
        _ffn_down_kernel,
        grid=(t // tm, f // tk),
        in_specs=[pl.BlockSpec((tm, tk), lambda i, kk: (i, kk)),
                  pl.BlockSpec((tk, d), lambda i, kk: (kk, 0)),
                  row,
                  pl.BlockSpec((1, d), lambda i, kk: (0, 0))],
        out_specs=row,
        out_shape=jax.ShapeDtypeStruct((t, d), F32),
        compiler_params=_cparams("arbitrary", "arbitrary"),
        name="ffn_down",
    )(hidden, wd, x1, npost.reshape(1, d))


def _largest_tile(t, cap):
    tm = min(t, cap)
    while t % tm:
        tm //= 2
    return tm


def _layer(x, p, lambda_init, cache):
    b, s, d = x.shape
    t = b * s
    n_heads = d // V_DIM
    x2 = x.reshape(t, d)
    tm_big = _largest_tile(t, 1024)
    tm_mid = _largest_tile(t, 512)

    qk_w = n_heads * 2 * HEAD_DIM
    col_k, col_v = qk_w, 2 * qk_w
    col_u = col_v + n_heads * V_DIM
    col_g = col_u + d
    col_a = col_g + d
    col_b = col_a + d
    w_in = p["w_in"]
    tn = 1024
    h, k_f32, k_bf = _norm_proj_k(x2, p["norm_mix_pre"], w_in, col_k, qk_w, tm_mid)
    (q,) = _proj(h, w_in, 0, qk_w, tm_mid, qk_w, scale=np.float32(HEAD_DIM ** -0.5 * LOG2E))

    slopes = 2.0 ** (-8.0 * jnp.arange(1, n_heads + 1, dtype=F32) / n_heads)
    lam_vecs = jnp.stack([p["lambda_q1"], p["lambda_k1"], p["lambda_q2"], p["lambda_k2"]])
    q3, k3 = q.reshape(b, s, -1), k_bf.reshape(b, s, -1)
    if cache is None:
        v_f32, vt = _proj_vt(h, w_in, col_v, b, s, n_heads, tm_mid)
        attn = _attn_prompt(q3, k3, vt, slopes, lam_vecs, p["subln_w"], n_heads, lambda_init,
                            _largest_tile(s, 512))
        rows_per_stream = GM_CHUNK
        ws_t = p["gm_ws"]
        bs_t = p["gm_bs"]
    else:
        ck, cv, layer = cache
        past = ck.shape[2]
        v_f32, v_bf = _proj(h, w_in, col_v, n_heads * V_DIM, tm_big, tn, want_f32=True)
        v3 = v_bf.reshape(b, s, -1)
        attn = _attn_sample(q3, k3, v3, ck.reshape(-1, HEAD_DIM), cv.reshape(-1, past, d), layer,
                            slopes, lam_vecs, p["subln_w"], n_heads, lambda_init)
        rows_per_stream = s
        reps = GM_CHUNK // s
        ws_t = jnp.tile(p["gm_ws"][:, :s, :s], (1, reps, reps))
        bs_t = jnp.tile(p["gm_bs"][:, :s], (1, reps))
    attn = attn.reshape(t, -1)

    res = _gmlp(h, w_in, col_u, col_g, p["gm_ln_w"], p["gm_ln_b"], ws_t,
                bs_t.reshape(GM_GROUPS, GM_CHUNK, 1), _largest_tile(t, 256), rows_per_stream,
                want_vn=cache is not None)
    gm = res[0]
    vn = res[1] if cache is not None else None

    merged = _merge(h, attn, gm, w_in, col_a, col_b, p["w_branch_attn"], p["w_branch_gmlp"],
                    tm_big, 512)
    x1, h2 = _out_proj(merged, p["w_out"], x2, p["norm_mix_post"], p["norm_ffn_pre"], tm_mid)
    hidden = _ffn_hidden(h2, p["w_ffn_gate"], p["w_ffn_up"], tm_big, 512)
    y = _ffn_down(hidden, p["w_ffn_down"], x1, p["norm_ffn_post"], tm_big, 512)
    return y.reshape(b, s, d), k_f32, v_f32, vn


def kernel(x_prompt, x_sample, cache_k, cache_v, norm_mix_pre, norm_mix_post, w_in, lambda_q1, lambda_k1, lambda_q2, lambda_k2, subln_w, gm_ln_w, gm_ln_b, gm_ws, gm_bs, w_branch_attn, w_branch_gmlp, w_out, norm_ffn_pre, norm_ffn_post, w_ffn_gate, w_ffn_up, w_ffn_down):
    depth = w_in.shape[0]
    bp, sp, d = x_prompt.shape
    bs_, ss, _ = x_sample.shape
    n_heads = d // V_DIM
    y_p, y_s = x_prompt, x_sample
    kp, vp, ks, vs, gs = [], [], [], [], []
    for l in range(depth):
        p = dict(norm_mix_pre=norm_mix_pre[l], norm_mix_post=norm_mix_post[l],
                 w_in=w_in[l].astype(BF16),
                 lambda_q1=lambda_q1[l], lambda_k1=lambda_k1[l], lambda_q2=lambda_q2[l],
                 lambda_k2=lambda_k2[l], subln_w=subln_w[l], gm_ln_w=gm_ln_w[l], gm_ln_b=gm_ln_b[l],
                 gm_ws=gm_ws[l], gm_bs=gm_bs[l],
                 w_branch_attn=w_branch_attn[l].astype(BF16),
                 w_branch_gmlp=w_branch_gmlp[l].astype(BF16),
                 w_out=w_out[l].astype(BF16),
                 norm_ffn_pre=norm_ffn_pre[l], norm_ffn_post=norm_ffn_post[l],
                 w_ffn_gate=w_ffn_gate[l].astype(BF16), w_ffn_up=w_ffn_up[l].astype(BF16),
                 w_ffn_down=w_ffn_down[l].astype(BF16))
        lambda_init = 0.8 - 0.6 * math.exp(-0.3 * l)
        y_p, k1, v1, _ = _layer(y_p, p, lambda_init, None)
        y_s, k2, v2, g2 = _layer(y_s, p, lambda_init, (cache_k, cache_v, l))
        kp.append(k1.reshape(bp, sp, n_heads, 2, HEAD_DIM))
        vp.append(v1.reshape(bp, sp, n_heads, V_DIM))
        ks.append(k2.reshape(bs_, ss, n_heads, 2, HEAD_DIM))
        vs.append(v2.reshape(bs_, ss, n_heads, V_DIM))
        gs.append(g2.reshape(bs_, ss, GM_GROUPS, d // GM_GROUPS))
    return (y_p, y_s, jnp.stack(kp), jnp.stack(vp), jnp.stack(ks), jnp.stack(vs), jnp.stack(gs))
```
